```python
import jax, jax.numpy as jnp
from jax import lax
import numpy as np

D_MODEL = 1024
BATCH = 1
SEQ = 16384
DEPTH = 2

CHUNK = 64
N_META = 16
CONV_WIDTH = 31
POOL_WINDOWS = (2, 4, 8, 16)
N_POOL_GROUPS = len(POOL_WINDOWS)
POOL_GROUP = D_MODEL // N_POOL_GROUPS
D_FF = 2816
N_EXPERTS = 8
TOP_K = 2
D_FF_EXPERT = 3584
N_MIX_A = (DEPTH + 1) // 2
N_MIX_B = DEPTH // 2
N_DENSE = (DEPTH + 1) // 2
N_MOE = DEPTH // 2
RMS_EPS = 1e-6
LN_EPS = 1e-5

kernel_name = "hybrid_conv_pool_moe_encoder"


def rms_norm(x, g):
    xf = x.astype(jnp.float32)
    y = xf * lax.rsqrt(jnp.mean(xf * xf, axis=-1, keepdims=True) + RMS_EPS)
    return y.astype(x.dtype) * g


def layer_norm(x, g, b):
    xf = x.astype(jnp.float32)
    mu = jnp.mean(xf, axis=-1, keepdims=True)
    var = jnp.mean(jnp.square(xf - mu), axis=-1, keepdims=True)
    y = (xf - mu) * lax.rsqrt(var + LN_EPS)
    return y.astype(x.dtype) * g + b


def conformer_conv(h, w_pw1, b_pw1, w_dw, b_dw, ln_g, ln_b, w_pw2, b_pw2):
    u = h @ w_pw1 + b_pw1
    a, gate = jnp.split(u, 2, axis=-1)
    u = a * jax.nn.sigmoid(gate)
    u = lax.conv_general_dilated(
        u, w_dw[:, None, :].astype(u.dtype),
        window_strides=(1,), padding=((CONV_WIDTH - 1, 0),),
        dimension_numbers=("NWC", "WIO", "NWC"),
        feature_group_count=D_MODEL) + b_dw
    u = jax.nn.silu(layer_norm(u, ln_g, ln_b))
    return u @ w_pw2 + b_pw2


def multiscale_pool(h, w_group, scale):
    B, L, _ = h.shape
    hg = h.reshape(B, L, N_POOL_GROUPS, POOL_GROUP).astype(jnp.float32)
    pos = jnp.arange(L)
    outs = []
    for gi, w in enumerate(POOL_WINDOWS):
        xg = hg[:, :, gi, :]
        cs = jnp.cumsum(xg, axis=1)
        cs_shift = jnp.pad(cs, ((0, 0), (w, 0), (0, 0)))[:, :L]
        count = jnp.minimum(pos + 1, w).astype(jnp.float32)[None, :, None]
        outs.append((cs - cs_shift) / count - xg)
    pooled = jnp.stack(outs, axis=2).astype(h.dtype)
    mixed = jnp.einsum("blgc,gcd->blgd", pooled, w_group)
    return mixed.reshape(B, L, D_MODEL) * scale


def swiglu(h, w_gate, w_up, w_down):
    return (jax.nn.silu(h @ w_gate) * (h @ w_up)) @ w_down


def moe_swiglu(h, w_router, e_gate, e_up, e_down):
    logits = (h @ w_router).astype(jnp.float32)
    top_vals, top_idx = lax.top_k(logits, TOP_K)
    top_w = jax.nn.softmax(top_vals, axis=-1)
    combine = jnp.sum(jax.nn.one_hot(top_idx, N_EXPERTS, dtype=jnp.float32)
                      * top_w[..., None], axis=-2).astype(h.dtype)
    y = jnp.zeros_like(h)
    for e in range(N_EXPERTS):
        y = y + combine[..., e:e + 1] * swiglu(h, e_gate[e], e_up[e], e_down[e])
    return y


def setup_inputs(seed: int = 0) -> dict:
    key = jax.random.key(seed)
    ks = iter(jax.random.split(key, 32))
    D = D_MODEL

    def nrm(shape, fan_in):
        return jax.random.normal(next(ks), shape, jnp.float32) * (fan_in ** -0.5)

    def gain(shape):
        return 1.0 + 0.05 * jax.random.normal(next(ks), shape, jnp.float32)

    def bias(shape):
        return 0.02 * jax.random.normal(next(ks), shape, jnp.float32)

    return {
        "x": jax.random.normal(next(ks), (BATCH, SEQ, D), jnp.float32),
        "meta_tokens": jax.random.normal(next(ks), (N_META, D), jnp.float32),
        "conv_w_pw1": nrm((N_MIX_A, D, 2 * D), D),
        "conv_b_pw1": bias((N_MIX_A, 2 * D)),
        "conv_w_dw": nrm((N_MIX_A, CONV_WIDTH, D), CONV_WIDTH),
        "conv_b_dw": bias((N_MIX_A, D)),
        "conv_ln_g": gain((N_MIX_A, D)),
        "conv_ln_b": bias((N_MIX_A, D)),
        "conv_w_pw2": nrm((N_MIX_A, D, D), D),
        "conv_b_pw2": bias((N_MIX_A, D)),
        "pool_w_group": nrm((N_MIX_B, N_POOL_GROUPS, POOL_GROUP, POOL_GROUP), POOL_GROUP),
        "pool_scale": gain((N_MIX_B, D)),
        "ffn_w_gate": nrm((N_DENSE, D, D_FF), D),
        "ffn_w_up": nrm((N_DENSE, D, D_FF), D),
        "ffn_w_down": nrm((N_DENSE, D_FF, D), D_FF),
        "moe_w_router": nrm((N_MOE, D, N_EXPERTS), D),
        "moe_w_gate": nrm((N_MOE, N_EXPERTS, D, D_FF_EXPERT), D),
        "moe_w_up": nrm((N_MOE, N_EXPERTS, D, D_FF_EXPERT), D),
        "moe_w_down": nrm((N_MOE, N_EXPERTS, D_FF_EXPERT, D), D_FF_EXPERT),
        "mix_norm_g": gain((DEPTH, D)),
        "ffn_norm_g": gain((DEPTH, D)),
        "final_norm_g": gain((D,)),
    }


def reference(x, meta_tokens,
              conv_w_pw1, conv_b_pw1, conv_w_dw, conv_b_dw, conv_ln_g, conv_ln_b,
              conv_w_pw2, conv_b_pw2,
              pool_w_group, pool_scale,
              ffn_w_gate, ffn_w_up, ffn_w_down,
              moe_w_router, moe_w_gate, moe_w_up, moe_w_down,
              mix_norm_g, ffn_norm_g, final_norm_g):
    B = x.shape[0]
    meta = jnp.broadcast_to(meta_tokens[None].astype(x.dtype), (B, N_META, D_MODEL))
    h = jnp.concatenate([meta, x], axis=1)

    for i in range(DEPTH):
        j = i // 2
        hn = rms_norm(h, mix_norm_g[i])
        if i % 2 == 0:
            h = h + conformer_conv(hn, conv_w_pw1[j], conv_b_pw1[j], conv_w_dw[j],
                                   conv_b_dw[j], conv_ln_g[j], conv_ln_b[j],
                                   conv_w_pw2[j], conv_b_pw2[j])
        else:
            h = h + multiscale_pool(hn, pool_w_group[j], pool_scale[j])
        hn = rms_norm(h, ffn_norm_g[i])
        if i % 2 == 0:
            h = h + swiglu(hn, ffn_w_gate[j], ffn_w_up[j], ffn_w_down[j])
        else:
            h = h + moe_swiglu(hn, moe_w_router[j], moe_w_gate[j], moe_w_up[j], moe_w_down[j])

    h = h[:, N_META:]
    return rms_norm(h, final_norm_g)
```

```python
import functools

import jax
import jax.numpy as jnp
from jax import lax
from jax.experimental import pallas as pl
from jax.experimental.pallas import tpu as pltpu

D = 1024
SEQ = 16384
N_META = 16
L = SEQ + N_META
CONV_WIDTH = 31
POOL_WINDOWS = (2, 4, 8, 16)
POOL_GROUP = D // len(POOL_WINDOWS)
D_FF = 2816
N_EXPERTS = 8
D_FF_EXPERT = 3584
RMS_EPS = 1e-6
LN_EPS = 1e-5

TM = 656
CONV_HALO = 32
CONV_ROWS = 16
POOL_HALO = 16
ROUTE_LANES = 128
GROUP_ROWS = 2048
SUB_ROWS = 512
FF_CHUNK = 512
N_FF_CHUNKS = D_FF_EXPERT // FF_CHUNK
N_ASSIGN = 2 * SEQ
N_GROUPS = N_ASSIGN // GROUP_ROWS + N_EXPERTS
TF = 512
VMEM_LIMIT = 56 * 1024 * 1024


def _rms(x, g):
    return x * lax.rsqrt(jnp.mean(x * x, axis=-1, keepdims=True) + RMS_EPS) * g


def _const_spec(shape):
    zeros = (0,) * len(shape)
    return pl.BlockSpec(shape, lambda *_: zeros, pipeline_mode=pl.Buffered(1))


def _conv_mixer_kernel(h_ref, g_ref, w1_ref, b1_ref, wdw_ref, bdw_ref, lng_ref, lnb_ref,
                       w2_ref, b2_ref, o_ref, ubuf, cbuf, wbc):
    i = pl.program_id(0)

    @pl.when(i == 0)
    def _():
        ubuf[:, 0:CONV_HALO, :] = jnp.zeros((D // 128, CONV_HALO, 128), jnp.float32)
        for k in range(CONV_WIDTH):
            wbc[k * 8:(k + 1) * 8, :] = jnp.broadcast_to(wdw_ref[k:k + 1, :], (8, D))

    h = h_ref[...]
    hn = _rms(h, g_ref[...]).astype(jnp.bfloat16)
    u = jnp.dot(hn, w1_ref[...], preferred_element_type=jnp.float32) + b1_ref[...]
    u = u[:, :D] * jax.nn.sigmoid(u[:, D:])
    for j in range(D // 128):
        ubuf[j, CONV_HALO:CONV_HALO + TM, :] = u[:, j * 128:(j + 1) * 128]

    for j in range(D // 128):
        lanes = slice(j * 128, (j + 1) * 128)
        taps = [wbc[k * 8:(k + 1) * 8, lanes] for k in range(CONV_WIDTH)]
        bias = jnp.broadcast_to(bdw_ref[:, lanes], (8, 128))

        def chunk(r, carry, j=j, lanes=lanes, taps=taps, bias=bias):
            base = pl.multiple_of(r * CONV_ROWS, CONV_ROWS)
            for q in range(CONV_ROWS // 8):
                acc = bias
                for k in range(CONV_WIDTH):
                    d = CONV_WIDTH - 1 - k
                    acc = acc + ubuf[j, pl.ds(CONV_HALO + base + q * 8 - d, 8), :] * taps[k]
                cbuf[pl.ds(base + q * 8, 8), lanes] = acc
            return carry

        lax.fori_loop(0, TM // CONV_ROWS, chunk, 0)
        ubuf[j, 0:CONV_HALO, :] = ubuf[j, TM:TM + CONV_HALO, :]

    c = cbuf[...]
    mu = jnp.mean(c, axis=-1, keepdims=True)
    cc = c - mu
    var = jnp.mean(cc * cc, axis=-1, keepdims=True)
    y = cc * lax.rsqrt(var + LN_EPS) * lng_ref[...] + lnb_ref[...]
    y = (y * jax.nn.sigmoid(y)).astype(jnp.bfloat16)
    o_ref[...] = h + jnp.dot(y, w2_ref[...], preferred_element_type=jnp.float32) + b2_ref[...]


def _conv_mixer(h, g, w1, b1, wdw, bdw, lng, lnb, w2, b2):
    row = lambda i: (i, 0)
    return pl.pallas_call(
        _conv_mixer_kernel,
        out_shape=jax.ShapeDtypeStruct((L, D), jnp.float32),
        grid=(L // TM,),
        in_specs=[
            pl.BlockSpec((TM, D), row),
            _const_spec((1, D)),
            _const_spec((D, 2 * D)),
            _const_spec((1, 2 * D)),
            _const_spec((CONV_WIDTH, D)),
            _const_spec((1, D)),
            _const_spec((1, D)),
            _const_spec((1, D)),
            _const_spec((D, D)),
            _const_spec((1, D)),
        ],
        out_specs=pl.BlockSpec((TM, D), row),
        scratch_shapes=[
            pltpu.VMEM((D // 128, CONV_HALO + TM, 128), jnp.float32),
            pltpu.VMEM((TM, D), jnp.float32),
            pltpu.VMEM((CONV_WIDTH * 8, D), jnp.float32),
        ],
        compiler_params=pltpu.CompilerParams(
            dimension_semantics=("arbitrary",), vmem_limit_bytes=VMEM_LIMIT),
        name="conv_mixer",
    )(h, g, w1, b1, wdw, bdw, lng, lnb, w2, b2)


def _dense_ffn_kernel(h_ref, g_ref, wg_ref, wu_ref, wd_ref, o_ref):
    h = h_ref[...]
    hn = _rms(h, g_ref[...]).astype(jnp.bfloat16)
    a = jnp.dot(hn, wg_ref[...], preferred_element_type=jnp.float32)
    b = jnp.dot(hn, wu_ref[...], preferred_element_type=jnp.float32)
    mid = (a * jax.nn.sigmoid(a) * b).astype(jnp.bfloat16)
    o_ref[...] = h + jnp.dot(mid, wd_ref[...], preferred_element_type=jnp.float32)


def _dense_ffn(h, g, wg, wu, wd):
    row = lambda i: (i, 0)
    return pl.pallas_call(
        _dense_ffn_kernel,
        out_shape=jax.ShapeDtypeStruct((L, D), jnp.float32),
        grid=(L // TM,),
        in_specs=[
            pl.BlockSpec((TM, D), row),
            _const_spec((1, D)),
            _const_spec((D, D_FF)),
            _const_spec((D, D_FF)),
            _const_spec((D_FF, D)),
        ],
        out_specs=pl.BlockSpec((TM, D), row),
        compiler_params=pltpu.CompilerParams(
            dimension_semantics=("arbitrary",), vmem_limit_bytes=VMEM_LIMIT),
        name="dense_ffn",
    )(h, g, wg, wu, wd)


def _pool_router_kernel(h_ref, gm_ref, wp_ref, sc_ref, gf_ref, wr_ref, o_ref, r_ref, pbuf):
    i = pl.program_id(0)

    @pl.when(i == 0)
    def _():
        pbuf[0:POOL_HALO, :] = jnp.zeros((POOL_HALO, D), jnp.float32)

    h = h_ref[...]
    hn = _rms(h, gm_ref[...])
    pbuf[POOL_HALO:POOL_HALO + TM, :] = hn
    pos = i * TM + lax.broadcasted_iota(jnp.int32, (TM, 1), 0)
    mixed = []
    for gi, w in enumerate(POOL_WINDOWS):
        lo = gi * POOL_GROUP
        s = hn[:, lo:lo + POOL_GROUP]
        for d in range(1, w):
            s = s + pbuf[POOL_HALO - d:POOL_HALO - d + TM, lo:lo + POOL_GROUP]
        inv = 1.0 / jnp.minimum(pos + 1, w).astype(jnp.float32)
        pooled = (s * inv - hn[:, lo:lo + POOL_GROUP]).astype(jnp.bfloat16)
        mixed.append(jnp.dot(pooled, wp_ref[gi], preferred_element_type=jnp.float32))
    pbuf[0:POOL_HALO, :] = pbuf[TM:TM + POOL_HALO, :]
    h = h + jnp.concatenate(mixed, axis=-1) * sc_ref[...]
    o_ref[...] = h

    hn2 = _rms(h, gf_ref[...])
    logits = jnp.dot(hn2, wr_ref[...], preferred_element_type=jnp.float32,
                     precision=lax.Precision.HIGHEST)
    lane = lax.broadcasted_iota(jnp.int32, (TM, ROUTE_LANES), 1)
    neg = jnp.float32(-jnp.inf)
    lg = jnp.where(lane < N_EXPERTS, logits, neg)
    m1 = jnp.max(lg, axis=-1, keepdims=True)
    i1 = jnp.min(jnp.where(lg == m1, lane, ROUTE_LANES), axis=-1, keepdims=True)
    lg2 = jnp.where(lane == i1, neg, lg)
    m2 = jnp.max(lg2, axis=-1, keepdims=True)
    i2 = jnp.min(jnp.where(lg2 == m2, lane, ROUTE_LANES), axis=-1, keepdims=True)
    ex = jnp.exp(m2 - m1)
    w1 = 1.0 / (1.0 + ex)
    w2 = ex / (1.0 + ex)
    r = jnp.where(lane == 0, w1, 0.0)
    r = jnp.where(lane == 1, w2, r)
    r = jnp.where(lane == 2, i1.astype(jnp.float32), r)
    r = jnp.where(lane == 3, i2.astype(jnp.float32), r)
    r_ref[...] = r


def _pool_router(h, gm, wp, sc, gf, wr):
    row = lambda i: (i, 0)
    return pl.pallas_call(
        _pool_router_kernel,
        out_shape=(jax.ShapeDtypeStruct((L, D), jnp.float32),
                   jax.ShapeDtypeStruct((L, ROUTE_LANES), jnp.float32)),
        grid=(L // TM,),
        in_specs=[
            pl.BlockSpec((TM, D), row),
            _const_spec((1, D)),
            _const_spec((len(POOL_WINDOWS), POOL_GROUP, POOL_GROUP)),
            _const_spec((1, D)),
            _const_spec((1, D)),
            _const_spec((D, ROUTE_LANES)),
        ],
        out_specs=(pl.BlockSpec((TM, D), row), pl.BlockSpec((TM, ROUTE_LANES), row)),
        scratch_shapes=[pltpu.VMEM((POOL_HALO + TM, D), jnp.float32)],
        compiler_params=pltpu.CompilerParams(
            dimension_semantics=("arbitrary",), vmem_limit_bytes=VMEM_LIMIT),
        name="pool_router",
    )(h, gm, wp, sc, gf, wr)


def _moe_kernel(gexp_ref, gstart_ref, grows_ref, order_ref,
                h_hbm, g_ref, wg_ref, wu_ref, wd_ref, y_hbm,
                xg, xb, acc, wgb, wub, wdb, gsem, ssem):
    del gexp_ref
    g = pl.program_id(0)
    c = pl.program_id(1)
    rows = grows_ref[g]
    start = gstart_ref[g]
    nsub = (rows + SUB_ROWS - 1) // SUB_ROWS

    @pl.when((g == 0) & (c == 0))
    def _():
        xg[...] = jnp.zeros((GROUP_ROWS, D), jnp.float32)

    def gather_copy(j):
        t = order_ref[start + j] & (SEQ - 1)
        return pltpu.make_async_copy(h_hbm.at[pl.ds(N_META + t, 1)], xg.at[pl.ds(j, 1)], gsem)

    def scatter_copy(j):
        a = order_ref[start + j]
        return pltpu.make_async_copy(
            acc.at[pl.ds(j, 1)], y_hbm.at[a >> 14, pl.ds(a & (SEQ - 1), 1)], ssem)

    @pl.when((c == 0) & (rows > 0))
    def _():
        def issue(j, carry):
            gather_copy(j).start()
            return carry
        lax.fori_loop(0, rows, issue, 0)

        def wait(j, carry):
            gather_copy(j).wait()
            return carry
        lax.fori_loop(0, rows, wait, 0)

        def norm(s, carry):
            r0 = pl.multiple_of(s * SUB_ROWS, SUB_ROWS)
            xb[pl.ds(r0, SUB_ROWS), :] = _rms(xg[pl.ds(r0, SUB_ROWS), :], g_ref[...]).astype(jnp.bfloat16)
            acc[pl.ds(r0, SUB_ROWS), :] = jnp.zeros((SUB_ROWS, D), jnp.float32)
            return carry
        lax.fori_loop(0, nsub, norm, 0)

    @pl.when(rows > 0)
    def _():
        wgb[...] = wg_ref[0].astype(jnp.bfloat16)
        wub[...] = wu_ref[0].astype(jnp.bfloat16)
        wdb[...] = wd_ref[0].astype(jnp.bfloat16)

        def sub(s, carry):
            r0 = pl.multiple_of(s * SUB_ROWS, SUB_ROWS)
            x = xb[pl.ds(r0, SUB_ROWS), :]
            a = jnp.dot(x, wgb[...], preferred_element_type=jnp.float32)
            b = jnp.dot(x, wub[...], preferred_element_type=jnp.float32)
            mid = (a * jax.nn.sigmoid(a) * b).astype(jnp.bfloat16)
            acc[pl.ds(r0, SUB_ROWS), :] += jnp.dot(mid, wdb[...], preferred_element_type=jnp.float32)
            return carry
        lax.fori_loop(0, nsub, sub, 0)

    @pl.when((c == N_FF_CHUNKS - 1) & (rows > 0))
    def _():
        def issue(j, carry):
            scatter_copy(j).start()
            return carry
        lax.fori_loop(0, rows, issue, 0)

        def wait(j, carry):
            scatter_copy(j).wait()
            return carry
        lax.fori_loop(0, rows, wait, 0)


def _moe(gexp, gstart, grows, order, h, g, wg, wu, wd):
    def w_in(gi, ci, gexp, gstart, grows, order):
        return (gexp[gi], 0, jnp.where(grows[gi] > 0, ci, N_FF_CHUNKS - 1))

    def w_out(gi, ci, gexp, gstart, grows, order):
        return (gexp[gi], jnp.where(grows[gi] > 0, ci, N_FF_CHUNKS - 1), 0)

    grid_spec = pltpu.PrefetchScalarGridSpec(
        num_scalar_prefetch=4,
        grid=(N_GROUPS, N_FF_CHUNKS),
        in_specs=[
            pl.BlockSpec(memory_space=pl.ANY),
            pl.BlockSpec((1, D), lambda gi, ci, *_: (0, 0)),
            pl.BlockSpec((1, D, FF_CHUNK), w_in),
            pl.BlockSpec((1, D, FF_CHUNK), w_in),
            pl.BlockSpec((1, FF_CHUNK, D), w_out),
        ],
        out_specs=pl.BlockSpec(memory_space=pl.ANY),
        scratch_shapes=[
            pltpu.VMEM((GROUP_ROWS, D), jnp.float32),
            pltpu.VMEM((GROUP_ROWS, D), jnp.bfloat16),
            pltpu.VMEM((GROUP_ROWS, D), jnp.float32),
            pltpu.VMEM((D, FF_CHUNK), jnp.bfloat16),
            pltpu.VMEM((D, FF_CHUNK), jnp.bfloat16),
            pltpu.VMEM((FF_CHUNK, D), jnp.bfloat16),
            pltpu.SemaphoreType.DMA(()),
            pltpu.SemaphoreType.DMA(()),
        ],
    )
    return pl.pallas_call(
        _moe_kernel,
        out_shape=jax.ShapeDtypeStruct((2, SEQ, D), jnp.float32),
        grid_spec=grid_spec,
        compiler_params=pltpu.CompilerParams(
            dimension_semantics=("arbitrary", "arbitrary"), vmem_limit_bytes=VMEM_LIMIT),
        name="moe_experts",
    )(gexp, gstart, grows, order, h, g, wg, wu, wd)


def _final_kernel(h_ref, y_ref, r_ref, g_ref, o_ref):
    r = r_ref[...]
    h = h_ref[...] + r[:, 0:1] * y_ref[0] + r[:, 1:2] * y_ref[1]
    o_ref[...] = _rms(h, g_ref[...])


def _final(h, y, r, g):
    row = lambda i: (i, 0)
    return pl.pallas_call(
        _final_kernel,
        out_shape=jax.ShapeDtypeStruct((SEQ, D), jnp.float32),
        grid=(SEQ // TF,),
        in_specs=[
            pl.BlockSpec((TF, D), row),
            pl.BlockSpec((2, TF, D), lambda i: (0, i, 0)),
            pl.BlockSpec((TF, ROUTE_LANES), row),
            _const_spec((1, D)),
        ],
        out_specs=pl.BlockSpec((TF, D), row),
        compiler_params=pltpu.CompilerParams(
            dimension_semantics=("arbitrary",), vmem_limit_bytes=VMEM_LIMIT),
        name="combine_norm",
    )(h, y, r, g)


def _group_table(flat_e):
    order = jnp.argsort(flat_e, stable=True).astype(jnp.int32)
    counts = jnp.sum(flat_e[:, None] == jnp.arange(N_EXPERTS, dtype=jnp.int32)[None, :],
                     axis=0, dtype=jnp.int32)
    ngrp = (counts + GROUP_ROWS - 1) // GROUP_ROWS
    cum_grp = jnp.cumsum(ngrp)
    total = cum_grp[-1]
    first_row = jnp.cumsum(counts) - counts
    gi = jnp.arange(N_GROUPS, dtype=jnp.int32)
    valid = gi < total
    e = jnp.searchsorted(cum_grp, jnp.minimum(gi, total - 1), side="right").astype(jnp.int32)
    e = jnp.minimum(e, N_EXPERTS - 1)
    local = gi - (cum_grp[e] - ngrp[e])
    gstart = jnp.where(valid, first_row[e] + local * GROUP_ROWS, 0).astype(jnp.int32)
    grows = jnp.where(valid, jnp.clip(counts[e] - local * GROUP_ROWS, 0, GROUP_ROWS), 0).astype(jnp.int32)
    return order, e, gstart, grows


def kernel(x, meta_tokens, conv_w_pw1, conv_b_pw1, conv_w_dw, conv_b_dw, conv_ln_g, conv_ln_b,
           conv_w_pw2, conv_b_pw2, pool_w_group, pool_scale, ffn_w_gate, ffn_w_up, ffn_w_down,
           moe_w_router, moe_w_gate, moe_w_up, moe_w_down, mix_norm_g, ffn_norm_g, final_norm_g):
    bf = jnp.bfloat16
    h = jnp.concatenate([meta_tokens.astype(x.dtype), x[0]], axis=0)

    h = _conv_mixer(h, mix_norm_g[0:1], conv_w_pw1[0].astype(bf), conv_b_pw1, conv_w_dw[0],
                    conv_b_dw, conv_ln_g, conv_ln_b, conv_w_pw2[0].astype(bf), conv_b_pw2)
    h = _dense_ffn(h, ffn_norm_g[0:1], ffn_w_gate[0].astype(bf), ffn_w_up[0].astype(bf),
                   ffn_w_down[0].astype(bf))

    wr = jnp.pad(moe_w_router[0], ((0, 0), (0, ROUTE_LANES - N_EXPERTS)))
    h, route = _pool_router(h, mix_norm_g[1:2], pool_w_group[0].astype(bf), pool_scale,
                            ffn_norm_g[1:2], wr)

    route = route[N_META:]
    flat_e = jnp.concatenate([route[:, 2], route[:, 3]]).astype(jnp.int32)
    order, gexp, gstart, grows = _group_table(flat_e)
    y = _moe(gexp, gstart, grows, order, h, ffn_norm_g[1:2], moe_w_gate[0], moe_w_up[0], moe_w_down[0])

    out = _final(h[N_META:], y, route, final_norm_g[None, :])
    return out[None]
```

```python
import jax
import jax.numpy as jnp
from jax import lax
from jax.experimental import pallas as pl
from jax.experimental.pallas import tpu as pltpu

D = 1024
SEQ = 16384
N_META = 16
L = SEQ + N_META
CONV_WIDTH = 31
POOL_WINDOWS = (2, 4, 8, 16)
POOL_GROUP = D // len(POOL_WINDOWS)
D_FF = 2816
N_EXPERTS = 8
D_FF_EXPERT = 3584
RMS_EPS = 1e-6
LN_EPS = 1e-5

LANES = 128
SUBLANES = 8
N_LB = D // LANES
TM = 656
CONV_HALO = 32
CONV_ROWS = 16
CONV_PARTS = 4
POOL_HALO = 16
ROUTE_LANES = LANES
GROUP_ROWS = 2048
SUB_ROWS = 512
N_SUB = GROUP_ROWS // SUB_ROWS
FF_CHUNK = 512
N_FF_CHUNKS = D_FF_EXPERT // FF_CHUNK
N_ASSIGN = 2 * SEQ
N_GROUPS = N_ASSIGN // GROUP_ROWS + N_EXPERTS
DMA_UNROLL = 8
TF = 512
VMEM_LIMIT = 56 * 1024 * 1024


def _rms(x, g):
    return x * lax.rsqrt(jnp.mean(x * x, axis=-1, keepdims=True) + RMS_EPS) * g


def _const_spec(shape):
    zeros = (0,) * len(shape)
    return pl.BlockSpec(shape, lambda *_: zeros, pipeline_mode=pl.Buffered(1))


def _to_token_tiles(ref, row0, x):
    n = x.shape[0]
    for j in range(N_LB):
        ref[pl.ds(row0 * SUBLANES + j, n, stride=SUBLANES), :] = x[:, j * LANES:(j + 1) * LANES]


def _from_token_tiles(ref, row0, n):
    return jnp.concatenate(
        [ref[pl.ds(row0 * SUBLANES + j, n, stride=SUBLANES), :] for j in range(N_LB)], axis=1)


def _conv_mixer_kernel(x_ref, meta_ref, g_ref, w1_ref, b1_ref, wdw_ref, bdw_ref, lng_ref, lnb_ref,
                       w2_ref, b2_ref, o_ref, hbuf, ubuf, cbuf, wbc):
    i = pl.program_id(0)

    @pl.when(i == 0)
    def _():
        ubuf[:, 0:CONV_HALO, :] = jnp.zeros((N_LB, CONV_HALO, LANES), jnp.float32)
        for k in range(CONV_WIDTH):
            wbc[k * SUBLANES:(k + 1) * SUBLANES, :] = jnp.broadcast_to(wdw_ref[k:k + 1, :], (SUBLANES, D))
        hbuf[0:N_META, :] = meta_ref[...]
        hbuf[N_META:TM, :] = x_ref[0:TM - N_META, :]

    @pl.when(i > 0)
    def _():
        hbuf[...] = x_ref[...]

    h = hbuf[...]
    hn = _rms(h, g_ref[...]).astype(jnp.bfloat16)
    u = jnp.dot(hn, w1_ref[...], preferred_element_type=jnp.float32) + b1_ref[...]
    u = u[:, :D] * jax.nn.sigmoid(u[:, D:])
    for j in range(N_LB):
        ubuf[j, CONV_HALO:CONV_HALO + TM, :] = u[:, j * LANES:(j + 1) * LANES]

    for j in range(N_LB):
        lanes = slice(j * LANES, (j + 1) * LANES)
        taps = [wbc[k * SUBLANES:(k + 1) * SUBLANES, lanes] for k in range(CONV_WIDTH)]
        bias = jnp.broadcast_to(bdw_ref[:, lanes], (SUBLANES, LANES))

        def chunk(r, carry, j=j, lanes=lanes, taps=taps, bias=bias):
            base = pl.multiple_of(r * CONV_ROWS, CONV_ROWS)
            for q in range(CONV_ROWS // SUBLANES):
                parts = [None] * CONV_PARTS
                for k in range(CONV_WIDTH):
                    d = CONV_WIDTH - 1 - k
                    term = ubuf[j, pl.ds(CONV_HALO + base + q * SUBLANES - d, SUBLANES), :] * taps[k]
                    parts[k % CONV_PARTS] = term if parts[k % CONV_PARTS] is None else parts[k % CONV_PARTS] + term
                cbuf[pl.ds(base + q * SUBLANES, SUBLANES), lanes] = (
                    (parts[0] + parts[1]) + (parts[2] + parts[3]) + bias)
            return carry

        lax.fori_loop(0, TM // CONV_ROWS, chunk, 0)
        ubuf[j, 0:CONV_HALO, :] = ubuf[j, TM:TM + CONV_HALO, :]

    c = cbuf[...]
    mu = jnp.mean(c, axis=-1, keepdims=True)
    cc = c - mu
    var = jnp.mean(cc * cc, axis=-1, keepdims=True)
    y = cc * lax.rsqrt(var + LN_EPS) * lng_ref[...] + lnb_ref[...]
    y = (y * jax.nn.sigmoid(y)).astype(jnp.bfloat16)
    o_ref[...] = h + jnp.dot(y, w2_ref[...], preferred_element_type=jnp.float32) + b2_ref[...]


def _conv_mixer(x, meta, g, w1, b1, wdw, bdw, lng, lnb, w2, b2):
    row = lambda i: (i, 0)
    return pl.pallas_call(
        _conv_mixer_kernel,
        out_shape=jax.ShapeDtypeStruct((L, D), jnp.float32),
        grid=(L // TM,),
        in_specs=[
            pl.BlockSpec((pl.Element(TM), pl.Element(D)),
                         lambda i: (SUBLANES * jnp.maximum(i * (TM // SUBLANES) - N_META // SUBLANES, 0), 0)),
            _const_spec((N_META, D)),
            _const_spec((1, D)),
            _const_spec((D, 2 * D)),
            _const_spec((1, 2 * D)),
            _const_spec((CONV_WIDTH, D)),
            _const_spec((1, D)),
            _const_spec((1, D)),
            _const_spec((1, D)),
            _const_spec((D, D)),
            _const_spec((1, D)),
        ],
        out_specs=pl.BlockSpec((TM, D), row),
        scratch_shapes=[
            pltpu.VMEM((TM, D), jnp.float32),
            pltpu.VMEM((N_LB, CONV_HALO + TM, LANES), jnp.float32),
            pltpu.VMEM((TM, D), jnp.float32),
            pltpu.VMEM((CONV_WIDTH * SUBLANES, D), jnp.float32),
        ],
        compiler_params=pltpu.CompilerParams(
            dimension_semantics=("arbitrary",), vmem_limit_bytes=VMEM_LIMIT),
        name="conv_mixer",
    )(x, meta, g, w1, b1, wdw, bdw, lng, lnb, w2, b2)


def _dense_ffn_kernel(h_ref, g_ref, wg_ref, wu_ref, wd_ref, o_ref):
    h = h_ref[...]
    hn = _rms(h, g_ref[...]).astype(jnp.bfloat16)
    a = jnp.dot(hn, wg_ref[...], preferred_element_type=jnp.float32)
    b = jnp.dot(hn, wu_ref[...], preferred_element_type=jnp.float32)
    mid = (a * jax.nn.sigmoid(a) * b).astype(jnp.bfloat16)
    o_ref[...] = h + jnp.dot(mid, wd_ref[...], preferred_element_type=jnp.float32)


def _dense_ffn(h, g, wg, wu, wd):
    row = lambda i: (i, 0)
    return pl.pallas_call(
        _dense_ffn_kernel,
        out_shape=jax.ShapeDtypeStruct((L, D), jnp.float32),
        grid=(L // TM,),
        in_specs=[
            pl.BlockSpec((TM, D), row),
            _const_spec((1, D)),
            _const_spec((D, D_FF)),
            _const_spec((D, D_FF)),
            _const_spec((D_FF, D)),
        ],
        out_specs=pl.BlockSpec((TM, D), row),
        compiler_params=pltpu.CompilerParams(
            dimension_semantics=("arbitrary",), vmem_limit_bytes=VMEM_LIMIT),
        name="dense_ffn",
    )(h, g, wg, wu, wd)


def _pool_router_kernel(h_ref, gm_ref, wp_ref, sc_ref, gf_ref, wrh_ref, wrl_ref, o_ref, r_ref, pbuf):
    i = pl.program_id(0)

    @pl.when(i == 0)
    def _():
        pbuf[:, 0:POOL_HALO, :] = jnp.zeros((N_LB, POOL_HALO, LANES), jnp.float32)

    h = h_ref[...]
    hn = _rms(h, gm_ref[...])
    for j in range(N_LB):
        pbuf[j, POOL_HALO:POOL_HALO + TM, :] = hn[:, j * LANES:(j + 1) * LANES]
    pos = i * TM + lax.broadcasted_iota(jnp.int32, (TM, 1), 0)
    lb_per_group = POOL_GROUP // LANES
    mixed = []
    for gi, w in enumerate(POOL_WINDOWS):
        inv = 1.0 / jnp.minimum(pos + 1, w).astype(jnp.float32)
        cols = []
        for j in range(gi * lb_per_group, (gi + 1) * lb_per_group):
            x = hn[:, j * LANES:(j + 1) * LANES]
            s = x
            for d in range(1, w):
                s = s + pbuf[j, POOL_HALO - d:POOL_HALO - d + TM, :]
            cols.append(s * inv - x)
        pooled = jnp.concatenate(cols, axis=1).astype(jnp.bfloat16)
        mixed.append(jnp.dot(pooled, wp_ref[gi], preferred_element_type=jnp.float32))
    for j in range(N_LB):
        pbuf[j, 0:POOL_HALO, :] = pbuf[j, TM:TM + POOL_HALO, :]
    h = h + jnp.concatenate(mixed, axis=-1) * sc_ref[...]
    _to_token_tiles(o_ref, 0, h)

    hn2 = _rms(h, gf_ref[...])
    xh = hn2.astype(jnp.bfloat16)
    xl = (hn2 - xh.astype(jnp.float32)).astype(jnp.bfloat16)
    logits = (jnp.dot(xh, wrh_ref[...], preferred_element_type=jnp.float32)
              + jnp.dot(xl, wrh_ref[...], preferred_element_type=jnp.float32)
              + jnp.dot(xh, wrl_ref[...], preferred_element_type=jnp.float32))
    lane = lax.broadcasted_iota(jnp.int32, (TM, ROUTE_LANES), 1)
    neg = jnp.float32(-jnp.inf)
    lg = jnp.where(lane < N_EXPERTS, logits, neg)
    m1 = jnp.max(lg, axis=-1, keepdims=True)
    i1 = jnp.min(jnp.where(lg == m1, lane, ROUTE_LANES), axis=-1, keepdims=True)
    lg2 = jnp.where(lane == i1, neg, lg)
    m2 = jnp.max(lg2, axis=-1, keepdims=True)
    i2 = jnp.min(jnp.where(lg2 == m2, lane, ROUTE_LANES), axis=-1, keepdims=True)
    ex = jnp.exp(m2 - m1)
    w1 = 1.0 / (1.0 + ex)
    w2 = ex / (1.0 + ex)
    r = jnp.where(lane == 0, w1, 0.0)
    r = jnp.where(lane == 1, w2, r)
    r = jnp.where(lane == 2, i1.astype(jnp.float32), r)
    r = jnp.where(lane == 3, i2.astype(jnp.float32), r)
    r_ref[...] = r


def _pool_router(h, gm, wp, sc, gf, wrh, wrl):
    row = lambda i: (i, 0)
    return pl.pallas_call(
        _pool_router_kernel,
        out_shape=(jax.ShapeDtypeStruct((L * SUBLANES, LANES), jnp.float32),
                   jax.ShapeDtypeStruct((L, ROUTE_LANES), jnp.float32)),
        grid=(L // TM,),
        in_specs=[
            pl.BlockSpec((TM, D), row),
            _const_spec((1, D)),
            _const_spec((len(POOL_WINDOWS), POOL_GROUP, POOL_GROUP)),
            _const_spec((1, D)),
            _const_spec((1, D)),
            _const_spec((D, ROUTE_LANES)),
            _const_spec((D, ROUTE_LANES)),
        ],
        out_specs=(pl.BlockSpec((TM * SUBLANES, LANES), row), pl.BlockSpec((TM, ROUTE_LANES), row)),
        scratch_shapes=[pltpu.VMEM((N_LB, POOL_HALO + TM, LANES), jnp.float32)],
        compiler_params=pltpu.CompilerParams(
            dimension_semantics=("arbitrary",), vmem_limit_bytes=VMEM_LIMIT),
        name="pool_router",
    )(h, gm, wp, sc, gf, wrh, wrl)


def _moe_kernel(gexp_ref, gstart_ref, grows_ref, order_ref,
                h_hbm, g_ref, wg_ref, wu_ref, wd_ref, y_hbm,
                xg, xb, acc, ys, wgb, wub, wdb, gsem, ssem):
    del gexp_ref
    g = pl.program_id(0)
    c = pl.program_id(1)
    rows = grows_ref[g]
    start = gstart_ref[g]
    nsub = (rows + SUB_ROWS - 1) // SUB_ROWS
    first = c == 0
    last = c == N_FF_CHUNKS - 1
    tile = lambda r: pl.ds(pl.multiple_of(r * SUBLANES, SUBLANES), SUBLANES)

    def gather_copy(st, j):
        t = order_ref[st + j] & (SEQ - 1)
        return pltpu.make_async_copy(h_hbm.at[tile(N_META + t)], xg.at[tile(j)], gsem)

    def scatter_copy(st, j):
        return pltpu.make_async_copy(ys.at[tile(j)], y_hbm.at[tile(order_ref[st + j])], ssem)

    def block_wait(sem):
        blk = pl.ds(0, DMA_UNROLL * SUBLANES)
        pltpu.make_async_copy(h_hbm.at[blk], xg.at[blk], sem).wait()

    def gather_start(grp):
        st = gstart_ref[grp]
        nblk = (grows_ref[grp] + DMA_UNROLL - 1) // DMA_UNROLL

        def body(b, carry):
            for u in range(DMA_UNROLL):
                gather_copy(st, b * DMA_UNROLL + u).start()
            return carry
        lax.fori_loop(0, nblk, body, 0)

    def gather_wait(grp):
        nblk = (grows_ref[grp] + DMA_UNROLL - 1) // DMA_UNROLL

        def body(b, carry):
            block_wait(gsem)
            return carry
        lax.fori_loop(0, nblk, body, 0)

    def scatter_start(grp):
        st = gstart_ref[grp]
        n = grows_ref[grp]
        nblk = n // DMA_UNROLL

        def body(b, carry):
            for u in range(DMA_UNROLL):
                scatter_copy(st, b * DMA_UNROLL + u).start()
            return carry
        lax.fori_loop(0, nblk, body, 0)

        def tail(j, carry):
            scatter_copy(st, j).start()
            return carry
        lax.fori_loop(nblk * DMA_UNROLL, n, tail, 0)

    def scatter_wait(grp):
        n = grows_ref[grp]
        nblk = n // DMA_UNROLL

        def body(b, carry):
            block_wait(ssem)
            return carry
        lax.fori_loop(0, nblk, body, 0)

        def tail(j, carry):
            scatter_copy(0, 0).wait()
            return carry
        lax.fori_loop(nblk * DMA_UNROLL, n, tail, 0)

    gprev = jnp.maximum(g - 1, 0)
    gnext = jnp.minimum(g + 1, N_GROUPS - 1)
    prev_live = (g > 0) & (grows_ref[gprev] > 0)

    @pl.when(first & (g == 0) & (rows > 0))
    def _():
        xg[...] = jnp.zeros((GROUP_ROWS * SUBLANES, LANES), jnp.float32)
        gather_start(0)

    @pl.when(first & (rows > 0))
    def _():
        gather_wait(g)

        def norm(s, carry):
            r0 = pl.multiple_of(s * SUB_ROWS, SUB_ROWS)
            x = _from_token_tiles(xg, r0, SUB_ROWS)
            xb[pl.ds(r0, SUB_ROWS), :] = _rms(x, g_ref[...]).astype(jnp.bfloat16)
            return carry
        lax.fori_loop(0, nsub, norm, 0)

        @pl.when((g + 1 < N_GROUPS) & (grows_ref[gnext] > 0))
        def _():
            gather_start(gnext)

    def expert_out(s):
        r0 = pl.multiple_of(s * SUB_ROWS, SUB_ROWS)
        x = xb[pl.ds(r0, SUB_ROWS), :]
        a = jnp.dot(x, wgb[...], preferred_element_type=jnp.float32)
        b = jnp.dot(x, wub[...], preferred_element_type=jnp.float32)
        mid = (a * jax.nn.sigmoid(a) * b).astype(jnp.bfloat16)
        return r0, jnp.dot(mid, wdb[...], preferred_element_type=jnp.float32)

    @pl.when(rows > 0)
    def _():
        wgb[...] = wg_ref[0].astype(jnp.bfloat16)
        wub[...] = wu_ref[0].astype(jnp.bfloat16)
        wdb[...] = wd_ref[0].astype(jnp.bfloat16)

    @pl.when(first & (rows > 0))
    def _():
        def sub(s, carry):
            r0, y = expert_out(s)
            acc[pl.ds(r0, SUB_ROWS), :] = y
            return carry
        lax.fori_loop(0, nsub, sub, 0)

    @pl.when(jnp.logical_not(first) & jnp.logical_not(last) & (rows > 0))
    def _():
        def sub(s, carry):
            r0, y = expert_out(s)
            acc[pl.ds(r0, SUB_ROWS), :] += y
            return carry
        lax.fori_loop(0, nsub, sub, 0)

    @pl.when(last & prev_live)
    def _():
        scatter_wait(gprev)

    @pl.when(last & (rows > 0))
    def _():
        def sub(s, carry):
            r0, y = expert_out(s)
            _to_token_tiles(ys, r0, acc[pl.ds(r0, SUB_ROWS), :] + y)
            return carry
        lax.fori_loop(0, nsub, sub, 0)
        scatter_start(g)

        @pl.when(g == N_GROUPS - 1)
        def _():
            scatter_wait(g)


def _moe(gexp, gstart, grows, order, h, g, wg, wu, wd):
    def w_in(gi, ci, gexp, gstart, grows, order):
        return (gexp[gi], 0, jnp.where(grows[gi] > 0, ci, N_FF_CHUNKS - 1))

    def w_out(gi, ci, gexp, gstart, grows, order):
        return (gexp[gi], jnp.where(grows[gi] > 0, ci, N_FF_CHUNKS - 1), 0)

    grid_spec = pltpu.PrefetchScalarGridSpec(
        num_scalar_prefetch=4,
        grid=(N_GROUPS, N_FF_CHUNKS),
        in_specs=[
            pl.BlockSpec(memory_space=pl.ANY),
            pl.BlockSpec((1, D), lambda gi, ci, *_: (0, 0)),
            pl.BlockSpec((1, D, FF_CHUNK), w_in),
            pl.BlockSpec((1, D, FF_CHUNK), w_in),
            pl.BlockSpec((1, FF_CHUNK, D), w_out),
        ],
        out_specs=pl.BlockSpec(memory_space=pl.ANY),
        scratch_shapes=[
            pltpu.VMEM((GROUP_ROWS * SUBLANES, LANES), jnp.float32),
            pltpu.VMEM((GROUP_ROWS, D), jnp.bfloat16),
            pltpu.VMEM((GROUP_ROWS, D), jnp.float32),
            pltpu.VMEM((GROUP_ROWS * SUBLANES, LANES), jnp.float32),
            pltpu.VMEM((D, FF_CHUNK), jnp.bfloat16),
            pltpu.VMEM((D, FF_CHUNK), jnp.bfloat16),
            pltpu.VMEM((FF_CHUNK, D), jnp.bfloat16),
            pltpu.SemaphoreType.DMA(()),
            pltpu.SemaphoreType.DMA(()),
        ],
    )
    return pl.pallas_call(
        _moe_kernel,
        out_shape=jax.ShapeDtypeStruct((N_ASSIGN * SUBLANES, LANES), jnp.float32),
        grid_spec=grid_spec,
        compiler_params=pltpu.CompilerParams(
            dimension_semantics=("arbitrary", "arbitrary"), vmem_limit_bytes=VMEM_LIMIT),
        name="moe_experts",
    )(gexp, gstart, grows, order, h, g, wg, wu, wd)


def _final_kernel(h_ref, y0_ref, y1_ref, r_ref, g_ref, o_ref):
    r = r_ref[...]
    h = (_from_token_tiles(h_ref, 0, TF)
         + r[:, 0:1] * _from_token_tiles(y0_ref, 0, TF)
         + r[:, 1:2] * _from_token_tiles(y1_ref, 0, TF))
    o_ref[...] = _rms(h, g_ref[...])


def _final(h, y, r, g):
    return pl.pallas_call(
        _final_kernel,
        out_shape=jax.ShapeDtypeStruct((SEQ, D), jnp.float32),
        grid=(SEQ // TF,),
        in_specs=[
            pl.BlockSpec((pl.Element(TF * SUBLANES), pl.Element(LANES)),
                         lambda i: ((N_META + i * TF) * SUBLANES, 0)),
            pl.BlockSpec((TF * SUBLANES, LANES), lambda i: (i, 0)),
            pl.BlockSpec((TF * SUBLANES, LANES), lambda i: (i + SEQ // TF, 0)),
            pl.BlockSpec((pl.Element(TF), pl.Element(ROUTE_LANES)),
                         lambda i: (SUBLANES * (N_META // SUBLANES + i * (TF // SUBLANES)), 0)),
            _const_spec((1, D)),
        ],
        out_specs=pl.BlockSpec((TF, D), lambda i: (i, 0)),
        compiler_params=pltpu.CompilerParams(
            dimension_semantics=("arbitrary",), vmem_limit_bytes=VMEM_LIMIT),
        name="combine_norm",
    )(h, y, y, r, g)


def _group_table(flat_e):
    order = jnp.argsort(flat_e, stable=True).astype(jnp.int32)
    order = jnp.concatenate([order, jnp.zeros((DMA_UNROLL,), jnp.int32)])
    experts = jnp.arange(N_EXPERTS, dtype=jnp.int32)
    counts = jnp.sum(flat_e[:, None] == experts[None, :], axis=0, dtype=jnp.int32)
    ngrp = (counts + GROUP_ROWS - 1) // GROUP_ROWS
    cum_grp = jnp.cumsum(ngrp)
    total = cum_grp[-1]
    first_row = jnp.cumsum(counts) - counts
    gi = jnp.arange(N_GROUPS, dtype=jnp.int32)
    valid = gi < total
    gclip = jnp.minimum(gi, total - 1)
    e = jnp.sum(gclip[:, None] >= cum_grp[None, :], axis=1, dtype=jnp.int32)
    e = jnp.clip(e, 0, N_EXPERTS - 1)
    local = gclip - (cum_grp[e] - ngrp[e])
    gstart = jnp.where(valid, first_row[e] + local * GROUP_ROWS, 0).astype(jnp.int32)
    grows = jnp.where(valid, jnp.clip(counts[e] - local * GROUP_ROWS, 0, GROUP_ROWS), 0).astype(jnp.int32)
    return order, e, gstart, grows


def kernel(x, meta_tokens, conv_w_pw1, conv_b_pw1, conv_w_dw, conv_b_dw, conv_ln_g, conv_ln_b,
           conv_w_pw2, conv_b_pw2, pool_w_group, pool_scale, ffn_w_gate, ffn_w_up, ffn_w_down,
           moe_w_router, moe_w_gate, moe_w_up, moe_w_down, mix_norm_g, ffn_norm_g, final_norm_g):
    bf = jnp.bfloat16
    h = _conv_mixer(x[0], meta_tokens.astype(x.dtype), mix_norm_g[0:1], conv_w_pw1[0].astype(bf),
                    conv_b_pw1, conv_w_dw[0], conv_b_dw, conv_ln_g, conv_ln_b,
                    conv_w_pw2[0].astype(bf), conv_b_pw2)
    h = _dense_ffn(h, ffn_norm_g[0:1], ffn_w_gate[0].astype(bf), ffn_w_up[0].astype(bf),
                   ffn_w_down[0].astype(bf))

    wr = jnp.pad(moe_w_router[0], ((0, 0), (0, ROUTE_LANES - N_EXPERTS)))
    wrh = wr.astype(bf)
    wrl = (wr - wrh.astype(jnp.float32)).astype(bf)
    ht, route = _pool_router(h, mix_norm_g[1:2], pool_w_group[0].astype(bf), pool_scale,
                             ffn_norm_g[1:2], wrh, wrl)

    flat_e = jnp.concatenate([route[N_META:, 2], route[N_META:, 3]]).astype(jnp.int32)
    order, gexp, gstart, grows = _group_table(flat_e)
    y = _moe(gexp, gstart, grows, order, ht, ffn_norm_g[1:2], moe_w_gate[0], moe_w_up[0], moe_w_down[0])

    out = _final(ht, y, route, final_norm_g[None, :])
    return out[None]
```

```python
import jax
import jax.numpy as jnp
from jax import lax
from jax.experimental import pallas as pl
from jax.experimental.pallas import tpu as pltpu

D = 1024
SEQ = 16384
N_META = 16
L = SEQ + N_META
CONV_WIDTH = 31
POOL_WINDOWS = (2, 4, 8, 16)
POOL_GROUP = D // len(POOL_WINDOWS)
D_FF = 2816
N_EXPERTS = 8
D_FF_EXPERT = 3584
RMS_EPS = 1e-6
LN_EPS = 1e-5

LANES = 128
SUBLANES = 8
N_LB = D // LANES
TM = 656
CONV_HALO = 32
CONV_ROWS = 16
CONV_PARTS = 4
POOL_HALO = 16
ROUTE_LANES = LANES
GROUP_ROWS = 2560
SUB_ROWS = 512
N_SUB = GROUP_ROWS // SUB_ROWS
FF_CHUNK = 512
N_FF_CHUNKS = D_FF_EXPERT // FF_CHUNK
N_ASSIGN = 2 * SEQ
N_GROUPS = -(-N_ASSIGN // GROUP_ROWS) + N_EXPERTS
DMA_UNROLL = 8
TF = 512
VMEM_LIMIT = 56 * 1024 * 1024


def _rms(x, g):
    return x * lax.rsqrt(jnp.mean(x * x, axis=-1, keepdims=True) + RMS_EPS) * g


def _const_spec(shape):
    zeros = (0,) * len(shape)
    return pl.BlockSpec(shape, lambda *_: zeros, pipeline_mode=pl.Buffered(1))


def _to_token_tiles(ref, row0, x):
    n = x.shape[0]
    for j in range(N_LB):
        ref[pl.ds(row0 * SUBLANES + j, n, stride=SUBLANES), :] = x[:, j * LANES:(j + 1) * LANES]


def _from_token_tiles(ref, row0, n):
    return jnp.concatenate(
        [ref[pl.ds(row0 * SUBLANES + j, n, stride=SUBLANES), :] for j in range(N_LB)], axis=1)


def _conv_mixer_kernel(x_ref, meta_ref, g_ref, w1_ref, b1_ref, wdw_ref, bdw_ref, lng_ref, lnb_ref,
                       w2_ref, b2_ref, o_ref, hbuf, ubuf, cbuf, wbc):
    i = pl.program_id(0)

    @pl.when(i == 0)
    def _():
        ubuf[:, 0:CONV_HALO, :] = jnp.zeros((N_LB, CONV_HALO, LANES), jnp.float32)
        for k in range(CONV_WIDTH):
            wbc[k * SUBLANES:(k + 1) * SUBLANES, :] = jnp.broadcast_to(wdw_ref[k:k + 1, :], (SUBLANES, D))
        hbuf[0:N_META, :] = meta_ref[...]
        hbuf[N_META:TM, :] = x_ref[0:TM - N_META, :]

    @pl.when(i > 0)
    def _():
        hbuf[...] = x_ref[...]

    h = hbuf[...]
    hn = _rms(h, g_ref[...]).astype(jnp.bfloat16)
    u = jnp.dot(hn, w1_ref[...], preferred_element_type=jnp.float32) + b1_ref[...]
    u = u[:, :D] * jax.nn.sigmoid(u[:, D:])
    for j in range(N_LB):
        ubuf[j, CONV_HALO:CONV_HALO + TM, :] = u[:, j * LANES:(j + 1) * LANES]

    for j in range(N_LB):
        lanes = slice(j * LANES, (j + 1) * LANES)
        taps = [wbc[k * SUBLANES:(k + 1) * SUBLANES, lanes] for k in range(CONV_WIDTH)]
        bias = jnp.broadcast_to(bdw_ref[:, lanes], (SUBLANES, LANES))

        def chunk(r, carry, j=j, lanes=lanes, taps=taps, bias=bias):
            base = pl.multiple_of(r * CONV_ROWS, CONV_ROWS)
            for q in range(CONV_ROWS // SUBLANES):
                parts = [None] * CONV_PARTS
                for k in range(CONV_WIDTH):
                    d = CONV_WIDTH - 1 - k
                    term = ubuf[j, pl.ds(CONV_HALO + base + q * SUBLANES - d, SUBLANES), :] * taps[k]
                    parts[k % CONV_PARTS] = term if parts[k % CONV_PARTS] is None else parts[k % CONV_PARTS] + term
                cbuf[pl.ds(base + q * SUBLANES, SUBLANES), lanes] = (
                    (parts[0] + parts[1]) + (parts[2] + parts[3]) + bias)
            return carry

        lax.fori_loop(0, TM // CONV_ROWS, chunk, 0)
        ubuf[j, 0:CONV_HALO, :] = ubuf[j, TM:TM + CONV_HALO, :]

    c = cbuf[...]
    mu = jnp.mean(c, axis=-1, keepdims=True)
    cc = c - mu
    var = jnp.mean(cc * cc, axis=-1, keepdims=True)
    y = cc * lax.rsqrt(var + LN_EPS) * lng_ref[...] + lnb_ref[...]
    y = (y * jax.nn.sigmoid(y)).astype(jnp.bfloat16)
    o_ref[...] = h + jnp.dot(y, w2_ref[...], preferred_element_type=jnp.float32) + b2_ref[...]


def _conv_mixer(x, meta, g, w1, b1, wdw, bdw, lng, lnb, w2, b2):
    row = lambda i: (i, 0)
    return pl.pallas_call(
        _conv_mixer_kernel,
        out_shape=jax.ShapeDtypeStruct((L, D), jnp.float32),
        grid=(L // TM,),
        in_specs=[
            pl.BlockSpec((pl.Element(TM), pl.Element(D)),
                         lambda i: (SUBLANES * jnp.maximum(i * (TM // SUBLANES) - N_META // SUBLANES, 0), 0)),
            _const_spec((N_META, D)),
            _const_spec((1, D)),
            _const_spec((D, 2 * D)),
            _const_spec((1, 2 * D)),
            _const_spec((CONV_WIDTH, D)),
            _const_spec((1, D)),
            _const_spec((1, D)),
            _const_spec((1, D)),
            _const_spec((D, D)),
            _const_spec((1, D)),
        ],
        out_specs=pl.BlockSpec((TM, D), row),
        scratch_shapes=[
            pltpu.VMEM((TM, D), jnp.float32),
            pltpu.VMEM((N_LB, CONV_HALO + TM, LANES), jnp.float32),
            pltpu.VMEM((TM, D), jnp.float32),
            pltpu.VMEM((CONV_WIDTH * SUBLANES, D), jnp.float32),
        ],
        compiler_params=pltpu.CompilerParams(
            dimension_semantics=("arbitrary",), vmem_limit_bytes=VMEM_LIMIT),
        name="conv_mixer",
    )(x, meta, g, w1, b1, wdw, bdw, lng, lnb, w2, b2)


def _dense_ffn_kernel(h_ref, g_ref, wg_ref, wu_ref, wd_ref, o_ref):
    h = h_ref[...]
    hn = _rms(h, g_ref[...]).astype(jnp.bfloat16)
    a = jnp.dot(hn, wg_ref[...], preferred_element_type=jnp.float32)
    b = jnp.dot(hn, wu_ref[...], preferred_element_type=jnp.float32)
    mid = (a * jax.nn.sigmoid(a) * b).astype(jnp.bfloat16)
    o_ref[...] = h + jnp.dot(mid, wd_ref[...], preferred_element_type=jnp.float32)


def _dense_ffn(h, g, wg, wu, wd):
    row = lambda i: (i, 0)
    return pl.pallas_call(
        _dense_ffn_kernel,
        out_shape=jax.ShapeDtypeStruct((L, D), jnp.float32),
        grid=(L // TM,),
        in_specs=[
            pl.BlockSpec((TM, D), row),
            _const_spec((1, D)),
            _const_spec((D, D_FF)),
            _const_spec((D, D_FF)),
            _const_spec((D_FF, D)),
        ],
        out_specs=pl.BlockSpec((TM, D), row),
        compiler_params=pltpu.CompilerParams(
            dimension_semantics=("arbitrary",), vmem_limit_bytes=VMEM_LIMIT),
        name="dense_ffn",
    )(h, g, wg, wu, wd)


def _pool_router_kernel(h_ref, gm_ref, wp_ref, sc_ref, gf_ref, wrh_ref, wrl_ref, o_ref, r_ref, pbuf):
    i = pl.program_id(0)

    @pl.when(i == 0)
    def _():
        pbuf[:, 0:POOL_HALO, :] = jnp.zeros((N_LB, POOL_HALO, LANES), jnp.float32)

    h = h_ref[...]
    hn = _rms(h, gm_ref[...])
    for j in range(N_LB):
        pbuf[j, POOL_HALO:POOL_HALO + TM, :] = hn[:, j * LANES:(j + 1) * LANES]
    pos = i * TM + lax.broadcasted_iota(jnp.int32, (TM, 1), 0)
    lb_per_group = POOL_GROUP // LANES
    mixed = []
    for gi, w in enumerate(POOL_WINDOWS):
        inv = 1.0 / jnp.minimum(pos + 1, w).astype(jnp.float32)
        cols = []
        for j in range(gi * lb_per_group, (gi + 1) * lb_per_group):
            x = hn[:, j * LANES:(j + 1) * LANES]
            s = x
            for d in range(1, w):
                s = s + pbuf[j, POOL_HALO - d:POOL_HALO - d + TM, :]
            cols.append(s * inv - x)
        pooled = jnp.concatenate(cols, axis=1).astype(jnp.bfloat16)
        mixed.append(jnp.dot(pooled, wp_ref[gi], preferred_element_type=jnp.float32))
    for j in range(N_LB):
        pbuf[j, 0:POOL_HALO, :] = pbuf[j, TM:TM + POOL_HALO, :]
    h = h + jnp.concatenate(mixed, axis=-1) * sc_ref[...]
    _to_token_tiles(o_ref, 0, h)

    hn2 = _rms(h, gf_ref[...])
    xh = hn2.astype(jnp.bfloat16)
    xl = (hn2 - xh.astype(jnp.float32)).astype(jnp.bfloat16)
    logits = (jnp.dot(xh, wrh_ref[...], preferred_element_type=jnp.float32)
              + jnp.dot(xl, wrh_ref[...], preferred_element_type=jnp.float32)
              + jnp.dot(xh, wrl_ref[...], preferred_element_type=jnp.float32))
    lane = lax.broadcasted_iota(jnp.int32, (TM, ROUTE_LANES), 1)
    neg = jnp.float32(-jnp.inf)
    lg = jnp.where(lane < N_EXPERTS, logits, neg)
    m1 = jnp.max(lg, axis=-1, keepdims=True)
    i1 = jnp.min(jnp.where(lg == m1, lane, ROUTE_LANES), axis=-1, keepdims=True)
    lg2 = jnp.where(lane == i1, neg, lg)
    m2 = jnp.max(lg2, axis=-1, keepdims=True)
    i2 = jnp.min(jnp.where(lg2 == m2, lane, ROUTE_LANES), axis=-1, keepdims=True)
    ex = jnp.exp(m2 - m1)
    w1 = 1.0 / (1.0 + ex)
    w2 = ex / (1.0 + ex)
    r = jnp.where(lane == 0, w1, 0.0)
    r = jnp.where(lane == 1, w2, r)
    r = jnp.where(lane == 2, i1.astype(jnp.float32), r)
    r = jnp.where(lane == 3, i2.astype(jnp.float32), r)
    r_ref[...] = r


def _pool_router(h, gm, wp, sc, gf, wrh, wrl):
    row = lambda i: (i, 0)
    return pl.pallas_call(
        _pool_router_kernel,
        out_shape=(jax.ShapeDtypeStruct((L * SUBLANES, LANES), jnp.float32),
                   jax.ShapeDtypeStruct((L, ROUTE_LANES), jnp.float32)),
        grid=(L // TM,),
        in_specs=[
            pl.BlockSpec((TM, D), row),
            _const_spec((1, D)),
            _const_spec((len(POOL_WINDOWS), POOL_GROUP, POOL_GROUP)),
            _const_spec((1, D)),
            _const_spec((1, D)),
            _const_spec((D, ROUTE_LANES)),
            _const_spec((D, ROUTE_LANES)),
        ],
        out_specs=(pl.BlockSpec((TM * SUBLANES, LANES), row), pl.BlockSpec((TM, ROUTE_LANES), row)),
        scratch_shapes=[pltpu.VMEM((N_LB, POOL_HALO + TM, LANES), jnp.float32)],
        compiler_params=pltpu.CompilerParams(
            dimension_semantics=("arbitrary",), vmem_limit_bytes=VMEM_LIMIT),
        name="pool_router",
    )(h, gm, wp, sc, gf, wrh, wrl)


def _moe_kernel(gexp_ref, gstart_ref, grows_ref, gsrc_ref, sdst_ref,
                h_hbm, g_ref, wg_ref, wu_ref, wd_ref, y_hbm,
                xg, xb, acc, ys, wgb, wub, wdb, gsem, ssem):
    del gexp_ref
    g = pl.program_id(0)
    c = pl.program_id(1)
    rows = grows_ref[g]
    start = gstart_ref[g]
    nsub = (rows + SUB_ROWS - 1) // SUB_ROWS
    first = c == 0
    last = c == N_FF_CHUNKS - 1
    tile = lambda r: pl.ds(pl.multiple_of(r * SUBLANES, SUBLANES), SUBLANES)

    rows8 = lambda v: pl.ds(pl.multiple_of(v, SUBLANES), SUBLANES)

    def scatter_copy(st, j):
        return pltpu.make_async_copy(ys.at[tile(j)], y_hbm.at[rows8(sdst_ref[st + j])], ssem)

    def block_wait(sem):
        blk = pl.ds(0, DMA_UNROLL * SUBLANES)
        pltpu.make_async_copy(h_hbm.at[blk], xg.at[blk], sem).wait()

    def gather_start(grp):
        st = gstart_ref[grp]
        nblk = (grows_ref[grp] + DMA_UNROLL - 1) // DMA_UNROLL

        def body(b, carry):
            base = st + b * DMA_UNROLL
            nrow = DMA_UNROLL * SUBLANES
            dst = xg.at[pl.ds(pl.multiple_of(b * nrow, nrow), nrow)]
            for u in range(DMA_UNROLL):
                pltpu.make_async_copy(h_hbm.at[rows8(gsrc_ref[base + u])],
                                      dst.at[pl.ds(u * SUBLANES, SUBLANES)], gsem).start()
            return carry
        lax.fori_loop(0, nblk, body, 0)

    def gather_wait(grp):
        nblk = (grows_ref[grp] + DMA_UNROLL - 1) // DMA_UNROLL

        def body(b, carry):
            block_wait(gsem)
            return carry
        lax.fori_loop(0, nblk, body, 0)

    def scatter_start(grp):
        st = gstart_ref[grp]
        n = grows_ref[grp]
        nblk = n // DMA_UNROLL

        def body(b, carry):
            for u in range(DMA_UNROLL):
                scatter_copy(st, b * DMA_UNROLL + u).start()
            return carry
        lax.fori_loop(0, nblk, body, 0)

        def tail(j, carry):
            scatter_copy(st, j).start()
            return carry
        lax.fori_loop(nblk * DMA_UNROLL, n, tail, 0)

    def scatter_wait(grp):
        n = grows_ref[grp]
        nblk = n // DMA_UNROLL

        def body(b, carry):
            block_wait(ssem)
            return carry
        lax.fori_loop(0, nblk, body, 0)

        def tail(j, carry):
            scatter_copy(0, 0).wait()
            return carry
        lax.fori_loop(nblk * DMA_UNROLL, n, tail, 0)

    gprev = jnp.maximum(g - 1, 0)
    gnext = jnp.minimum(g + 1, N_GROUPS - 1)
    prev_live = (g > 0) & (grows_ref[gprev] > 0)

    @pl.when(first & (g == 0) & (rows > 0))
    def _():
        xg[...] = jnp.zeros((GROUP_ROWS * SUBLANES, LANES), jnp.float32)
        gather_start(0)

    @pl.when(first & (rows > 0))
    def _():
        gather_wait(g)

        def norm(s, carry):
            r0 = pl.multiple_of(s * SUB_ROWS, SUB_ROWS)
            x = _from_token_tiles(xg, r0, SUB_ROWS)
            xb[pl.ds(r0, SUB_ROWS), :] = _rms(x, g_ref[...]).astype(jnp.bfloat16)
            return carry
        lax.fori_loop(0, nsub, norm, 0)

        @pl.when((g + 1 < N_GROUPS) & (grows_ref[gnext] > 0))
        def _():
            gather_start(gnext)

    def expert_out(s):
        r0 = pl.multiple_of(s * SUB_ROWS, SUB_ROWS)
        x = xb[pl.ds(r0, SUB_ROWS), :]
        a = jnp.dot(x, wgb[...], preferred_element_type=jnp.float32)
        b = jnp.dot(x, wub[...], preferred_element_type=jnp.float32)
        mid = (a * jax.nn.sigmoid(a) * b).astype(jnp.bfloat16)
        return r0, jnp.dot(mid, wdb[...], preferred_element_type=jnp.float32)

    @pl.when(rows > 0)
    def _():
        wgb[...] = wg_ref[0].astype(jnp.bfloat16)
        wub[...] = wu_ref[0].astype(jnp.bfloat16)
        wdb[...] = wd_ref[0].astype(jnp.bfloat16)

    @pl.when(first & (rows > 0))
    def _():
        def sub(s, carry):
            r0, y = expert_out(s)
            acc[pl.ds(r0, SUB_ROWS), :] = y
            return carry
        lax.fori_loop(0, nsub, sub, 0)

    @pl.when(jnp.logical_not(first) & jnp.logical_not(last) & (rows > 0))
    def _():
        def sub(s, carry):
            r0, y = expert_out(s)
            acc[pl.ds(r0, SUB_ROWS), :] += y
            return carry
        lax.fori_loop(0, nsub, sub, 0)

    @pl.when(last & prev_live)
    def _():
        scatter_wait(gprev)

    @pl.when(last & (rows > 0))
    def _():
        def sub(s, carry):
            r0, y = expert_out(s)
            _to_token_tiles(ys, r0, acc[pl.ds(r0, SUB_ROWS), :] + y)
            return carry
        lax.fori_loop(0, nsub, sub, 0)
        scatter_start(g)

        @pl.when(g == N_GROUPS - 1)
        def _():
            scatter_wait(g)


def _moe(gexp, gstart, grows, gsrc, sdst, h, g, wg, wu, wd):
    def w_in(gi, ci, gexp, gstart, grows, *_):
        return (gexp[gi], 0, jnp.where(grows[gi] > 0, ci, N_FF_CHUNKS - 1))

    def w_out(gi, ci, gexp, gstart, grows, *_):
        return (gexp[gi], jnp.where(grows[gi] > 0, ci, N_FF_CHUNKS - 1), 0)

    grid_spec = pltpu.PrefetchScalarGridSpec(
        num_scalar_prefetch=5,
        grid=(N_GROUPS, N_FF_CHUNKS),
        in_specs=[
            pl.BlockSpec(memory_space=pl.ANY),
            pl.BlockSpec((1, D), lambda gi, ci, *_: (0, 0)),
            pl.BlockSpec((1, D, FF_CHUNK), w_in),
            pl.BlockSpec((1, D, FF_CHUNK), w_in),
            pl.BlockSpec((1, FF_CHUNK, D), w_out),
        ],
        out_specs=pl.BlockSpec(memory_space=pl.ANY),
        scratch_shapes=[
            pltpu.VMEM((GROUP_ROWS * SUBLANES, LANES), jnp.float32),
            pltpu.VMEM((GROUP_ROWS, D), jnp.bfloat16),
            pltpu.VMEM((GROUP_ROWS, D), jnp.float32),
            pltpu.VMEM((GROUP_ROWS * SUBLANES, LANES), jnp.float32),
            pltpu.VMEM((D, FF_CHUNK), jnp.bfloat16),
            pltpu.VMEM((D, FF_CHUNK), jnp.bfloat16),
            pltpu.VMEM((FF_CHUNK, D), jnp.bfloat16),
            pltpu.SemaphoreType.DMA(()),
            pltpu.SemaphoreType.DMA(()),
        ],
    )
    return pl.pallas_call(
        _moe_kernel,
        out_shape=jax.ShapeDtypeStruct((N_ASSIGN * SUBLANES, LANES), jnp.float32),
        grid_spec=grid_spec,
        compiler_params=pltpu.CompilerParams(
            dimension_semantics=("arbitrary", "arbitrary"), vmem_limit_bytes=VMEM_LIMIT),
        name="moe_experts",
    )(gexp, gstart, grows, gsrc, sdst, h, g, wg, wu, wd)


def _final_kernel(h_ref, y0_ref, y1_ref, r_ref, g_ref, o_ref):
    r = r_ref[...]
    h = (_from_token_tiles(h_ref, 0, TF)
         + r[:, 0:1] * _from_token_tiles(y0_ref, 0, TF)
         + r[:, 1:2] * _from_token_tiles(y1_ref, 0, TF))
    o_ref[...] = _rms(h, g_ref[...])


def _final(h, y, r, g):
    return pl.pallas_call(
        _final_kernel,
        out_shape=jax.ShapeDtypeStruct((SEQ, D), jnp.float32),
        grid=(SEQ // TF,),
        in_specs=[
            pl.BlockSpec((pl.Element(TF * SUBLANES), pl.Element(LANES)),
                         lambda i: ((N_META + i * TF) * SUBLANES, 0)),
            pl.BlockSpec((TF * SUBLANES, LANES), lambda i: (i, 0)),
            pl.BlockSpec((TF * SUBLANES, LANES), lambda i: (i + SEQ // TF, 0)),
            pl.BlockSpec((pl.Element(TF), pl.Element(ROUTE_LANES)),
                         lambda i: (SUBLANES * (N_META // SUBLANES + i * (TF // SUBLANES)), 0)),
            _const_spec((1, D)),
        ],
        out_specs=pl.BlockSpec((TF, D), lambda i: (i, 0)),
        compiler_params=pltpu.CompilerParams(
            dimension_semantics=("arbitrary",), vmem_limit_bytes=VMEM_LIMIT),
        name="combine_norm",
    )(h, y, y, r, g)


def _group_table(flat_e):
    order = jnp.argsort(flat_e, stable=True).astype(jnp.int32)
    order = jnp.concatenate([order, jnp.zeros((DMA_UNROLL,), jnp.int32)])
    gsrc = (N_META + (order & (SEQ - 1))) * SUBLANES
    sdst = order * SUBLANES
    experts = jnp.arange(N_EXPERTS, dtype=jnp.int32)
    counts = jnp.sum(flat_e[:, None] == experts[None, :], axis=0, dtype=jnp.int32)
    ngrp = (counts + GROUP_ROWS - 1) // GROUP_ROWS
    cum_grp = jnp.cumsum(ngrp)
    total = cum_grp[-1]
    first_row = jnp.cumsum(counts) - counts
    gi = jnp.arange(N_GROUPS, dtype=jnp.int32)
    valid = gi < total
    gclip = jnp.minimum(gi, total - 1)
    e = jnp.sum(gclip[:, None] >= cum_grp[None, :], axis=1, dtype=jnp.int32)
    e = jnp.clip(e, 0, N_EXPERTS - 1)
    local = gclip - (cum_grp[e] - ngrp[e])
    gstart = jnp.where(valid, first_row[e] + local * GROUP_ROWS, 0).astype(jnp.int32)
    grows = jnp.where(valid, jnp.clip(counts[e] - local * GROUP_ROWS, 0, GROUP_ROWS), 0).astype(jnp.int32)
    return gsrc, sdst, e, gstart, grows


def kernel(x, meta_tokens, conv_w_pw1, conv_b_pw1, conv_w_dw, conv_b_dw, conv_ln_g, conv_ln_b,
           conv_w_pw2, conv_b_pw2, pool_w_group, pool_scale, ffn_w_gate, ffn_w_up, ffn_w_down,
           moe_w_router, moe_w_gate, moe_w_up, moe_w_down, mix_norm_g, ffn_norm_g, final_norm_g):
    bf = jnp.bfloat16
    h = _conv_mixer(x[0], meta_tokens.astype(x.dtype), mix_norm_g[0:1], conv_w_pw1[0].astype(bf),
                    conv_b_pw1, conv_w_dw[0], conv_b_dw, conv_ln_g, conv_ln_b,
                    conv_w_pw2[0].astype(bf), conv_b_pw2)
    h = _dense_ffn(h, ffn_norm_g[0:1], ffn_w_gate[0].astype(bf), ffn_w_up[0].astype(bf),
                   ffn_w_down[0].astype(bf))

    wr = jnp.pad(moe_w_router[0], ((0, 0), (0, ROUTE_LANES - N_EXPERTS)))
    wrh = wr.astype(bf)
    wrl = (wr - wrh.astype(jnp.float32)).astype(bf)
    ht, route = _pool_router(h, mix_norm_g[1:2], pool_w_group[0].astype(bf), pool_scale,
                             ffn_norm_g[1:2], wrh, wrl)

    flat_e = jnp.concatenate([route[N_META:, 2], route[N_META:, 3]]).astype(jnp.int32)
    gsrc, sdst, gexp, gstart, grows = _group_table(flat_e)
    y = _moe(gexp, gstart, grows, gsrc, sdst, ht, ffn_norm_g[1:2], moe_w_gate[0], moe_w_up[0], moe_w_down[0])

    out = _final(ht, y, route, final_norm_g[None, :])
    return out[None]
```

```python
import jax
import jax.numpy as jnp
from jax import lax
from jax.experimental import pallas as pl
from jax.experimental.pallas import tpu as pltpu

D = 1024
SEQ = 16384
N_META = 16
L = SEQ + N_META
CONV_WIDTH = 31
POOL_WINDOWS = (2, 4, 8, 16)
POOL_GROUP = D // len(POOL_WINDOWS)
D_FF = 2816
N_EXPERTS = 8
D_FF_EXPERT = 3584
RMS_EPS = 1e-6
LN_EPS = 1e-5

LANES = 128
SUBLANES = 8
N_LB = D // LANES
TM = 656
CONV_HALO = 32
CONV_ROWS = 16
CONV_PARTS = 4
POOL_HALO = 16
ROUTE_LANES = LANES
GROUP_ROWS = 2560
SUB_ROWS = 640
N_SUB = GROUP_ROWS // SUB_ROWS
FF_CHUNK = 512
N_FF_CHUNKS = D_FF_EXPERT // FF_CHUNK
N_ASSIGN = 2 * SEQ
N_GROUPS = -(-N_ASSIGN // GROUP_ROWS) + N_EXPERTS
DMA_UNROLL = 8
TF = 1024
VMEM_LIMIT = 56 * 1024 * 1024


def _rms(x, g):
    return x * lax.rsqrt(jnp.mean(x * x, axis=-1, keepdims=True) + RMS_EPS) * g


def _const_spec(shape):
    zeros = (0,) * len(shape)
    return pl.BlockSpec(shape, lambda *_: zeros, pipeline_mode=pl.Buffered(1))


def _to_token_tiles(ref, row0, x):
    n = x.shape[0]
    for j in range(N_LB):
        ref[pl.ds(row0 * SUBLANES + j, n, stride=SUBLANES), :] = x[:, j * LANES:(j + 1) * LANES]


def _from_token_tiles(ref, row0, n):
    return jnp.concatenate(
        [ref[pl.ds(row0 * SUBLANES + j, n, stride=SUBLANES), :] for j in range(N_LB)], axis=1)


def _conv_mixer_kernel(x_ref, meta_ref, g_ref, w1_ref, b1_ref, wdw_ref, bdw_ref, lng_ref, lnb_ref,
                       w2_ref, b2_ref, o_ref, hbuf, ubuf, cbuf, wbc):
    i = pl.program_id(0)

    @pl.when(i == 0)
    def _():
        ubuf[:, 0:CONV_HALO, :] = jnp.zeros((N_LB, CONV_HALO, LANES), jnp.float32)
        for k in range(CONV_WIDTH):
            wbc[k * SUBLANES:(k + 1) * SUBLANES, :] = jnp.broadcast_to(wdw_ref[k:k + 1, :], (SUBLANES, D))
        hbuf[0:N_META, :] = meta_ref[...]
        hbuf[N_META:TM, :] = x_ref[0:TM - N_META, :]

    @pl.when(i > 0)
    def _():
        hbuf[...] = x_ref[...]

    h = hbuf[...]
    hn = _rms(h, g_ref[...]).astype(jnp.bfloat16)
    u = jnp.dot(hn, w1_ref[...], preferred_element_type=jnp.float32) + b1_ref[...]
    u = u[:, :D] * jax.nn.sigmoid(u[:, D:])
    for j in range(N_LB):
        ubuf[j, CONV_HALO:CONV_HALO + TM, :] = u[:, j * LANES:(j + 1) * LANES]

    for j in range(N_LB):
        lanes = slice(j * LANES, (j + 1) * LANES)
        taps = [wbc[k * SUBLANES:(k + 1) * SUBLANES, lanes] for k in range(CONV_WIDTH)]
        bias = jnp.broadcast_to(bdw_ref[:, lanes], (SUBLANES, LANES))

        def chunk(r, carry, j=j, lanes=lanes, taps=taps, bias=bias):
            base = pl.multiple_of(r * CONV_ROWS, CONV_ROWS)
            for q in range(CONV_ROWS // SUBLANES):
                parts = [None] * CONV_PARTS
                for k in range(CONV_WIDTH):
                    d = CONV_WIDTH - 1 - k
                    term = ubuf[j, pl.ds(CONV_HALO + base + q * SUBLANES - d, SUBLANES), :] * taps[k]
                    parts[k % CONV_PARTS] = term if parts[k % CONV_PARTS] is None else parts[k % CONV_PARTS] + term
                cbuf[pl.ds(base + q * SUBLANES, SUBLANES), lanes] = (
                    (parts[0] + parts[1]) + (parts[2] + parts[3]) + bias)
            return carry

        lax.fori_loop(0, TM // CONV_ROWS, chunk, 0)
        ubuf[j, 0:CONV_HALO, :] = ubuf[j, TM:TM + CONV_HALO, :]

    c = cbuf[...]
    mu = jnp.mean(c, axis=-1, keepdims=True)
    cc = c - mu
    var = jnp.mean(cc * cc, axis=-1, keepdims=True)
    y = cc * lax.rsqrt(var + LN_EPS) * lng_ref[...] + lnb_ref[...]
    y = (y * jax.nn.sigmoid(y)).astype(jnp.bfloat16)
    o_ref[...] = h + jnp.dot(y, w2_ref[...], preferred_element_type=jnp.float32) + b2_ref[...]


def _conv_mixer(x, meta, g, w1, b1, wdw, bdw, lng, lnb, w2, b2):
    row = lambda i: (i, 0)
    return pl.pallas_call(
        _conv_mixer_kernel,
        out_shape=jax.ShapeDtypeStruct((L, D), jnp.float32),
        grid=(L // TM,),
        in_specs=[
            pl.BlockSpec((pl.Element(TM), pl.Element(D)),
                         lambda i: (SUBLANES * jnp.maximum(i * (TM // SUBLANES) - N_META // SUBLANES, 0), 0)),
            _const_spec((N_META, D)),
            _const_spec((1, D)),
            _const_spec((D, 2 * D)),
            _const_spec((1, 2 * D)),
            _const_spec((CONV_WIDTH, D)),
            _const_spec((1, D)),
            _const_spec((1, D)),
            _const_spec((1, D)),
            _const_spec((D, D)),
            _const_spec((1, D)),
        ],
        out_specs=pl.BlockSpec((TM, D), row),
        scratch_shapes=[
            pltpu.VMEM((TM, D), jnp.float32),
            pltpu.VMEM((N_LB, CONV_HALO + TM, LANES), jnp.float32),
            pltpu.VMEM((TM, D), jnp.float32),
            pltpu.VMEM((CONV_WIDTH * SUBLANES, D), jnp.float32),
        ],
        compiler_params=pltpu.CompilerParams(
            dimension_semantics=("arbitrary",), vmem_limit_bytes=VMEM_LIMIT),
        name="conv_mixer",
    )(x, meta, g, w1, b1, wdw, bdw, lng, lnb, w2, b2)


def _dense_ffn_kernel(h_ref, g_ref, wg_ref, wu_ref, wd_ref, o_ref):
    h = h_ref[...]
    hn = _rms(h, g_ref[...]).astype(jnp.bfloat16)
    a = jnp.dot(hn, wg_ref[...], preferred_element_type=jnp.float32)
    b = jnp.dot(hn, wu_ref[...], preferred_element_type=jnp.float32)
    mid = (a * jax.nn.sigmoid(a) * b).astype(jnp.bfloat16)
    o_ref[...] = h + jnp.dot(mid, wd_ref[...], preferred_element_type=jnp.float32)


def _dense_ffn(h, g, wg, wu, wd):
    row = lambda i: (i, 0)
    return pl.pallas_call(
        _dense_ffn_kernel,
        out_shape=jax.ShapeDtypeStruct((L, D), jnp.float32),
        grid=(L // TM,),
        in_specs=[
            pl.BlockSpec((TM, D), row),
            _const_spec((1, D)),
            _const_spec((D, D_FF)),
            _const_spec((D, D_FF)),
            _const_spec((D_FF, D)),
        ],
        out_specs=pl.BlockSpec((TM, D), row),
        compiler_params=pltpu.CompilerParams(
            dimension_semantics=("arbitrary",), vmem_limit_bytes=VMEM_LIMIT),
        name="dense_ffn",
    )(h, g, wg, wu, wd)


def _pool_router_kernel(h_ref, gm_ref, wp_ref, sc_ref, gf_ref, wrh_ref, wrl_ref, o_ref, r_ref, pbuf):
    i = pl.program_id(0)

    @pl.when(i == 0)
    def _():
        pbuf[:, 0:POOL_HALO, :] = jnp.zeros((N_LB, POOL_HALO, LANES), jnp.float32)

    h = h_ref[...]
    hn = _rms(h, gm_ref[...])
    for j in range(N_LB):
        pbuf[j, POOL_HALO:POOL_HALO + TM, :] = hn[:, j * LANES:(j + 1) * LANES]
    pos = i * TM + lax.broadcasted_iota(jnp.int32, (TM, 1), 0)
    lb_per_group = POOL_GROUP // LANES
    mixed = []
    for gi, w in enumerate(POOL_WINDOWS):
        inv = 1.0 / jnp.minimum(pos + 1, w).astype(jnp.float32)
        cols = []
        for j in range(gi * lb_per_group, (gi + 1) * lb_per_group):
            x = hn[:, j * LANES:(j + 1) * LANES]
            s = x
            for d in range(1, w):
                s = s + pbuf[j, POOL_HALO - d:POOL_HALO - d + TM, :]
            cols.append(s * inv - x)
        pooled = jnp.concatenate(cols, axis=1).astype(jnp.bfloat16)
        mixed.append(jnp.dot(pooled, wp_ref[gi], preferred_element_type=jnp.float32))
    for j in range(N_LB):
        pbuf[j, 0:POOL_HALO, :] = pbuf[j, TM:TM + POOL_HALO, :]
    h = h + jnp.concatenate(mixed, axis=-1) * sc_ref[...]
    _to_token_tiles(o_ref, 0, h)

    hn2 = _rms(h, gf_ref[...])
    xh = hn2.astype(jnp.bfloat16)
    xl = (hn2 - xh.astype(jnp.float32)).astype(jnp.bfloat16)
    logits = (jnp.dot(xh, wrh_ref[...], preferred_element_type=jnp.float32)
              + jnp.dot(xl, wrh_ref[...], preferred_element_type=jnp.float32)
              + jnp.dot(xh, wrl_ref[...], preferred_element_type=jnp.float32))
    lane = lax.broadcasted_iota(jnp.int32, (TM, ROUTE_LANES), 1)
    neg = jnp.float32(-jnp.inf)
    lg = jnp.where(lane < N_EXPERTS, logits, neg)
    m1 = jnp.max(lg, axis=-1, keepdims=True)
    i1 = jnp.min(jnp.where(lg == m1, lane, ROUTE_LANES), axis=-1, keepdims=True)
    lg2 = jnp.where(lane == i1, neg, lg)
    m2 = jnp.max(lg2, axis=-1, keepdims=True)
    i2 = jnp.min(jnp.where(lg2 == m2, lane, ROUTE_LANES), axis=-1, keepdims=True)
    ex = jnp.exp(m2 - m1)
    w1 = 1.0 / (1.0 + ex)
    w2 = ex / (1.0 + ex)
    r = jnp.where(lane == 0, w1, 0.0)
    r = jnp.where(lane == 1, w2, r)
    r = jnp.where(lane == 2, i1.astype(jnp.float32), r)
    r = jnp.where(lane == 3, i2.astype(jnp.float32), r)
    r_ref[...] = r


def _pool_router(h, gm, wp, sc, gf, wrh, wrl):
    row = lambda i: (i, 0)
    return pl.pallas_call(
        _pool_router_kernel,
        out_shape=(jax.ShapeDtypeStruct((L * SUBLANES, LANES), jnp.float32),
                   jax.ShapeDtypeStruct((L, ROUTE_LANES), jnp.float32)),
        grid=(L // TM,),
        in_specs=[
            pl.BlockSpec((TM, D), row),
            _const_spec((1, D)),
            _const_spec((len(POOL_WINDOWS), POOL_GROUP, POOL_GROUP)),
            _const_spec((1, D)),
            _const_spec((1, D)),
            _const_spec((D, ROUTE_LANES)),
            _const_spec((D, ROUTE_LANES)),
        ],
        out_specs=(pl.BlockSpec((TM * SUBLANES, LANES), row), pl.BlockSpec((TM, ROUTE_LANES), row)),
        scratch_shapes=[pltpu.VMEM((N_LB, POOL_HALO + TM, LANES), jnp.float32)],
        compiler_params=pltpu.CompilerParams(
            dimension_semantics=("arbitrary",), vmem_limit_bytes=VMEM_LIMIT),
        name="pool_router",
    )(h, gm, wp, sc, gf, wrh, wrl)


def _moe_kernel(gexp_ref, gstart_ref, grows_ref, gsrc_ref, sdst_ref,
                h_hbm, g_ref, wg_ref, wu_ref, wd_ref, y_hbm,
                xg, xb, acc, ys, wgb, wub, wdb, gsem, ssem):
    del gexp_ref
    g = pl.program_id(0)
    c = pl.program_id(1)
    rows = grows_ref[g]
    start = gstart_ref[g]
    nsub = (rows + SUB_ROWS - 1) // SUB_ROWS
    first = c == 0
    last = c == N_FF_CHUNKS - 1
    tile = lambda r: pl.ds(pl.multiple_of(r * SUBLANES, SUBLANES), SUBLANES)

    rows8 = lambda v: pl.ds(pl.multiple_of(v, SUBLANES), SUBLANES)

    def scatter_copy(st, j):
        return pltpu.make_async_copy(ys.at[tile(j)], y_hbm.at[rows8(sdst_ref[st + j])], ssem)

    def block_wait(sem):
        blk = pl.ds(0, DMA_UNROLL * SUBLANES)
        pltpu.make_async_copy(h_hbm.at[blk], xg.at[blk], sem).wait()

    def gather_start(grp):
        st = gstart_ref[grp]
        nblk = (grows_ref[grp] + DMA_UNROLL - 1) // DMA_UNROLL

        def body(b, carry):
            base = st + b * DMA_UNROLL
            nrow = DMA_UNROLL * SUBLANES
            dst = xg.at[pl.ds(pl.multiple_of(b * nrow, nrow), nrow)]
            for u in range(DMA_UNROLL):
                pltpu.make_async_copy(h_hbm.at[rows8(gsrc_ref[base + u])],
                                      dst.at[pl.ds(u * SUBLANES, SUBLANES)], gsem).start()
            return carry
        lax.fori_loop(0, nblk, body, 0)

    def gather_wait(grp):
        nblk = (grows_ref[grp] + DMA_UNROLL - 1) // DMA_UNROLL

        def body(b, carry):
            block_wait(gsem)
            return carry
        lax.fori_loop(0, nblk, body, 0)

    def scatter_start(grp):
        st = gstart_ref[grp]
        n = grows_ref[grp]
        nblk = n // DMA_UNROLL

        def body(b, carry):
            for u in range(DMA_UNROLL):
                scatter_copy(st, b * DMA_UNROLL + u).start()
            return carry
        lax.fori_loop(0, nblk, body, 0)

        def tail(j, carry):
            scatter_copy(st, j).start()
            return carry
        lax.fori_loop(nblk * DMA_UNROLL, n, tail, 0)

    def scatter_wait(grp):
        n = grows_ref[grp]
        nblk = n // DMA_UNROLL

        def body(b, carry):
            block_wait(ssem)
            return carry
        lax.fori_loop(0, nblk, body, 0)

        def tail(j, carry):
            scatter_copy(0, 0).wait()
            return carry
        lax.fori_loop(nblk * DMA_UNROLL, n, tail, 0)

    gprev = jnp.maximum(g - 1, 0)
    gnext = jnp.minimum(g + 1, N_GROUPS - 1)
    prev_live = (g > 0) & (grows_ref[gprev] > 0)

    @pl.when(first & (g == 0) & (rows > 0))
    def _():
        xg[...] = jnp.zeros((GROUP_ROWS * SUBLANES, LANES), jnp.float32)
        gather_start(0)

    @pl.when(first & (rows > 0))
    def _():
        gather_wait(g)

        def norm(s, carry):
            r0 = pl.multiple_of(s * SUB_ROWS, SUB_ROWS)
            x = _from_token_tiles(xg, r0, SUB_ROWS)
            xb[pl.ds(r0, SUB_ROWS), :] = _rms(x, g_ref[...]).astype(jnp.bfloat16)
            return carry
        lax.fori_loop(0, nsub, norm, 0)

        @pl.when((g + 1 < N_GROUPS) & (grows_ref[gnext] > 0))
        def _():
            gather_start(gnext)

    def expert_out(s):
        r0 = pl.multiple_of(s * SUB_ROWS, SUB_ROWS)
        x = xb[pl.ds(r0, SUB_ROWS), :]
        a = jnp.dot(x, wgb[...], preferred_element_type=jnp.float32)
        b = jnp.dot(x, wub[...], preferred_element_type=jnp.float32)
        mid = (a * jax.nn.sigmoid(a) * b).astype(jnp.bfloat16)
        return r0, jnp.dot(mid, wdb[...], preferred_element_type=jnp.float32)

    @pl.when(rows > 0)
    def _():
        wgb[...] = wg_ref[0].astype(jnp.bfloat16)
        wub[...] = wu_ref[0].astype(jnp.bfloat16)
        wdb[...] = wd_ref[0].astype(jnp.bfloat16)

    @pl.when(first & (rows > 0))
    def _():
        def sub(s, carry):
            r0, y = expert_out(s)
            acc[pl.ds(r0, SUB_ROWS), :] = y
            return carry
        lax.fori_loop(0, nsub, sub, 0)

    @pl.when(jnp.logical_not(first) & jnp.logical_not(last) & (rows > 0))
    def _():
        def sub(s, carry):
            r0, y = expert_out(s)
            acc[pl.ds(r0, SUB_ROWS), :] += y
            return carry
        lax.fori_loop(0, nsub, sub, 0)

    @pl.when(last & prev_live)
    def _():
        scatter_wait(gprev)

    @pl.when(last & (rows > 0))
    def _():
        def sub(s, carry):
            r0, y = expert_out(s)
            _to_token_tiles(ys, r0, acc[pl.ds(r0, SUB_ROWS), :] + y)
            return carry
        lax.fori_loop(0, nsub, sub, 0)
        scatter_start(g)

        @pl.when(g == N_GROUPS - 1)
        def _():
            scatter_wait(g)


def _moe(gexp, gstart, grows, gsrc, sdst, h, g, wg, wu, wd):
    def w_in(gi, ci, gexp, gstart, grows, *_):
        return (gexp[gi], 0, jnp.where(grows[gi] > 0, ci, N_FF_CHUNKS - 1))

    def w_out(gi, ci, gexp, gstart, grows, *_):
        return (gexp[gi], jnp.where(grows[gi] > 0, ci, N_FF_CHUNKS - 1), 0)

    grid_spec = pltpu.PrefetchScalarGridSpec(
        num_scalar_prefetch=5,
        grid=(N_GROUPS, N_FF_CHUNKS),
        in_specs=[
            pl.BlockSpec(memory_space=pl.ANY),
            pl.BlockSpec((1, D), lambda gi, ci, *_: (0, 0)),
            pl.BlockSpec((1, D, FF_CHUNK), w_in),
            pl.BlockSpec((1, D, FF_CHUNK), w_in),
            pl.BlockSpec((1, FF_CHUNK, D), w_out),
        ],
        out_specs=pl.BlockSpec(memory_space=pl.ANY),
        scratch_shapes=[
            pltpu.VMEM((GROUP_ROWS * SUBLANES, LANES), jnp.float32),
            pltpu.VMEM((GROUP_ROWS, D), jnp.bfloat16),
            pltpu.VMEM((GROUP_ROWS, D), jnp.float32),
            pltpu.VMEM((GROUP_ROWS * SUBLANES, LANES), jnp.float32),
            pltpu.VMEM((D, FF_CHUNK), jnp.bfloat16),
            pltpu.VMEM((D, FF_CHUNK), jnp.bfloat16),
            pltpu.VMEM((FF_CHUNK, D), jnp.bfloat16),
            pltpu.SemaphoreType.DMA(()),
            pltpu.SemaphoreType.DMA(()),
        ],
    )
    return pl.pallas_call(
        _moe_kernel,
        out_shape=jax.ShapeDtypeStruct((N_ASSIGN * SUBLANES, LANES), jnp.float32),
        grid_spec=grid_spec,
        compiler_params=pltpu.CompilerParams(
            dimension_semantics=("arbitrary", "arbitrary"), vmem_limit_bytes=VMEM_LIMIT),
        name="moe_experts",
    )(gexp, gstart, grows, gsrc, sdst, h, g, wg, wu, wd)


def _final_kernel(h_ref, y0_ref, y1_ref, r_ref, g_ref, o_ref):
    r = r_ref[...]
    h = (_from_token_tiles(h_ref, 0, TF)
         + r[:, 0:1] * _from_token_tiles(y0_ref, 0, TF)
         + r[:, 1:2] * _from_token_tiles(y1_ref, 0, TF))
    o_ref[...] = _rms(h, g_ref[...])


def _final(h, y, r, g):
    return pl.pallas_call(
        _final_kernel,
        out_shape=jax.ShapeDtypeStruct((SEQ, D), jnp.float32),
        grid=(SEQ // TF,),
        in_specs=[
            pl.BlockSpec((pl.Element(TF * SUBLANES), pl.Element(LANES)),
                         lambda i: ((N_META + i * TF) * SUBLANES, 0)),
            pl.BlockSpec((TF * SUBLANES, LANES), lambda i: (i, 0)),
            pl.BlockSpec((TF * SUBLANES, LANES), lambda i: (i + SEQ // TF, 0)),
            pl.BlockSpec((pl.Element(TF), pl.Element(ROUTE_LANES)),
                         lambda i: (SUBLANES * (N_META // SUBLANES + i * (TF // SUBLANES)), 0)),
            _const_spec((1, D)),
        ],
        out_specs=pl.BlockSpec((TF, D), lambda i: (i, 0)),
        compiler_params=pltpu.CompilerParams(
            dimension_semantics=("arbitrary",), vmem_limit_bytes=VMEM_LIMIT),
        name="combine_norm",
    )(h, y, y, r, g)


def _group_table(flat_e):
    order = jnp.argsort(flat_e, stable=True).astype(jnp.int32)
    order = jnp.concatenate([order, jnp.zeros((DMA_UNROLL,), jnp.int32)])
    gsrc = (N_META + (order & (SEQ - 1))) * SUBLANES
    sdst = order * SUBLANES
    experts = jnp.arange(N_EXPERTS, dtype=jnp.int32)
    counts = jnp.sum(flat_e[:, None] == experts[None, :], axis=0, dtype=jnp.int32)
    ngrp = (counts + GROUP_ROWS - 1) // GROUP_ROWS
    cum_grp = jnp.cumsum(ngrp)
    total = cum_grp[-1]
    first_row = jnp.cumsum(counts) - counts
    gi = jnp.arange(N_GROUPS, dtype=jnp.int32)
    valid = gi < total
    gclip = jnp.minimum(gi, total - 1)
    e = jnp.sum(gclip[:, None] >= cum_grp[None, :], axis=1, dtype=jnp.int32)
    e = jnp.clip(e, 0, N_EXPERTS - 1)
    local = gclip - (cum_grp[e] - ngrp[e])
    gstart = jnp.where(valid, first_row[e] + local * GROUP_ROWS, 0).astype(jnp.int32)
    grows = jnp.where(valid, jnp.clip(counts[e] - local * GROUP_ROWS, 0, GROUP_ROWS), 0).astype(jnp.int32)
    return gsrc, sdst, e, gstart, grows


def kernel(x, meta_tokens, conv_w_pw1, conv_b_pw1, conv_w_dw, conv_b_dw, conv_ln_g, conv_ln_b,
           conv_w_pw2, conv_b_pw2, pool_w_group, pool_scale, ffn_w_gate, ffn_w_up, ffn_w_down,
           moe_w_router, moe_w_gate, moe_w_up, moe_w_down, mix_norm_g, ffn_norm_g, final_norm_g):
    bf = jnp.bfloat16
    h = _conv_mixer(x[0], meta_tokens.astype(x.dtype), mix_norm_g[0:1], conv_w_pw1[0].astype(bf),
                    conv_b_pw1, conv_w_dw[0], conv_b_dw, conv_ln_g, conv_ln_b,
                    conv_w_pw2[0].astype(bf), conv_b_pw2)
    h = _dense_ffn(h, ffn_norm_g[0:1], ffn_w_gate[0].astype(bf), ffn_w_up[0].astype(bf),
                   ffn_w_down[0].astype(bf))

    wr = jnp.pad(moe_w_router[0], ((0, 0), (0, ROUTE_LANES - N_EXPERTS)))
    wrh = wr.astype(bf)
    wrl = (wr - wrh.astype(jnp.float32)).astype(bf)
    ht, route = _pool_router(h, mix_norm_g[1:2], pool_w_group[0].astype(bf), pool_scale,
                             ffn_norm_g[1:2], wrh, wrl)

    flat_e = jnp.concatenate([route[N_META:, 2], route[N_META:, 3]]).astype(jnp.int32)
    gsrc, sdst, gexp, gstart, grows = _group_table(flat_e)
    y = _moe(gexp, gstart, grows, gsrc, sdst, ht, ffn_norm_g[1:2], moe_w_gate[0], moe_w_up[0], moe_w_down[0])

    out = _final(ht, y, route, final_norm_g[None, :])
    return out[None]
```

```python
import jax
import jax.numpy as jnp
from jax import lax
from jax.experimental import pallas as pl
from jax.experimental.pallas import tpu as pltpu

D = 1024
SEQ = 16384
N_META = 16
L = SEQ + N_META
CONV_WIDTH = 31
POOL_WINDOWS = (2, 4, 8, 16)
POOL_GROUP = D // len(POOL_WINDOWS)
D_FF = 2816
N_EXPERTS = 8
D_FF_EXPERT = 3584
RMS_EPS = 1e-6
LN_EPS = 1e-5

LANES = 128
SUBLANES = 8
N_LB = D // LANES
TM = 656
CONV_HALO = 32
CONV_ROWS = 16
CONV_PARTS = 4
POOL_HALO = 16
ROUTE_LANES = LANES
GROUP_ROWS = 2160
SUB_ROWS = 720
N_SUB = GROUP_ROWS // SUB_ROWS
FF_CHUNK = 512
N_FF_CHUNKS = D_FF_EXPERT // FF_CHUNK
N_ASSIGN = 2 * SEQ
N_GROUPS = -(-N_ASSIGN // GROUP_ROWS) + N_EXPERTS
DMA_UNROLL = 8
TF = 1024
VMEM_LIMIT = 56 * 1024 * 1024


def _rms(x, g):
    return x * lax.rsqrt(jnp.mean(x * x, axis=-1, keepdims=True) + RMS_EPS) * g


def _const_spec(shape):
    zeros = (0,) * len(shape)
    return pl.BlockSpec(shape, lambda *_: zeros, pipeline_mode=pl.Buffered(1))


def _to_token_tiles(ref, row0, x):
    n = x.shape[0]
    for j in range(N_LB):
        ref[pl.ds(row0 * SUBLANES + j, n, stride=SUBLANES), :] = x[:, j * LANES:(j + 1) * LANES]


def _from_token_tiles(ref, row0, n):
    return jnp.concatenate(
        [ref[pl.ds(row0 * SUBLANES + j, n, stride=SUBLANES), :] for j in range(N_LB)], axis=1)


def _conv_mixer_kernel(x_ref, meta_ref, g_ref, w1_ref, b1_ref, wdw_ref, bdw_ref, lng_ref, lnb_ref,
                       w2_ref, b2_ref, o_ref, hbuf, ubuf, cbuf, wbc):
    i = pl.program_id(0)

    @pl.when(i == 0)
    def _():
        ubuf[:, 0:CONV_HALO, :] = jnp.zeros((N_LB, CONV_HALO, LANES), jnp.float32)
        for k in range(CONV_WIDTH):
            wbc[k * SUBLANES:(k + 1) * SUBLANES, :] = jnp.broadcast_to(wdw_ref[k:k + 1, :], (SUBLANES, D))
        hbuf[0:N_META, :] = meta_ref[...]
        hbuf[N_META:TM, :] = x_ref[0:TM - N_META, :]

    @pl.when(i > 0)
    def _():
        hbuf[...] = x_ref[...]

    h = hbuf[...]
    hn = _rms(h, g_ref[...]).astype(jnp.bfloat16)
    u = jnp.dot(hn, w1_ref[...], preferred_element_type=jnp.float32) + b1_ref[...]
    u = u[:, :D] * jax.nn.sigmoid(u[:, D:])
    for j in range(N_LB):
        ubuf[j, CONV_HALO:CONV_HALO + TM, :] = u[:, j * LANES:(j + 1) * LANES]

    for j in range(N_LB):
        lanes = slice(j * LANES, (j + 1) * LANES)
        taps = [wbc[k * SUBLANES:(k + 1) * SUBLANES, lanes] for k in range(CONV_WIDTH)]
        bias = jnp.broadcast_to(bdw_ref[:, lanes], (SUBLANES, LANES))

        def chunk(r, carry, j=j, lanes=lanes, taps=taps, bias=bias):
            base = pl.multiple_of(r * CONV_ROWS, CONV_ROWS)
            for q in range(CONV_ROWS // SUBLANES):
                parts = [None] * CONV_PARTS
                for k in range(CONV_WIDTH):
                    d = CONV_WIDTH - 1 - k
                    term = ubuf[j, pl.ds(CONV_HALO + base + q * SUBLANES - d, SUBLANES), :] * taps[k]
                    parts[k % CONV_PARTS] = term if parts[k % CONV_PARTS] is None else parts[k % CONV_PARTS] + term
                cbuf[pl.ds(base + q * SUBLANES, SUBLANES), lanes] = (
                    (parts[0] + parts[1]) + (parts[2] + parts[3]) + bias)
            return carry

        lax.fori_loop(0, TM // CONV_ROWS, chunk, 0)
        ubuf[j, 0:CONV_HALO, :] = ubuf[j, TM:TM + CONV_HALO, :]

    c = cbuf[...]
    mu = jnp.mean(c, axis=-1, keepdims=True)
    cc = c - mu
    var = jnp.mean(cc * cc, axis=-1, keepdims=True)
    y = cc * lax.rsqrt(var + LN_EPS) * lng_ref[...] + lnb_ref[...]
    y = (y * jax.nn.sigmoid(y)).astype(jnp.bfloat16)
    o_ref[...] = h + jnp.dot(y, w2_ref[...], preferred_element_type=jnp.float32) + b2_ref[...]


def _conv_mixer(x, meta, g, w1, b1, wdw, bdw, lng, lnb, w2, b2):
    row = lambda i: (i, 0)
    return pl.pallas_call(
        _conv_mixer_kernel,
        out_shape=jax.ShapeDtypeStruct((L, D), jnp.float32),
        grid=(L // TM,),
        in_specs=[
            pl.BlockSpec((pl.Element(TM), pl.Element(D)),
                         lambda i: (SUBLANES * jnp.maximum(i * (TM // SUBLANES) - N_META // SUBLANES, 0), 0)),
            _const_spec((N_META, D)),
            _const_spec((1, D)),
            _const_spec((D, 2 * D)),
            _const_spec((1, 2 * D)),
            _const_spec((CONV_WIDTH, D)),
            _const_spec((1, D)),
            _const_spec((1, D)),
            _const_spec((1, D)),
            _const_spec((D, D)),
            _const_spec((1, D)),
        ],
        out_specs=pl.BlockSpec((TM, D), row),
        scratch_shapes=[
            pltpu.VMEM((TM, D), jnp.float32),
            pltpu.VMEM((N_LB, CONV_HALO + TM, LANES), jnp.float32),
            pltpu.VMEM((TM, D), jnp.float32),
            pltpu.VMEM((CONV_WIDTH * SUBLANES, D), jnp.float32),
        ],
        compiler_params=pltpu.CompilerParams(
            dimension_semantics=("arbitrary",), vmem_limit_bytes=VMEM_LIMIT),
        name="conv_mixer",
    )(x, meta, g, w1, b1, wdw, bdw, lng, lnb, w2, b2)


def _dense_ffn_kernel(h_ref, g_ref, wg_ref, wu_ref, wd_ref, o_ref):
    h = h_ref[...]
    hn = _rms(h, g_ref[...]).astype(jnp.bfloat16)
    a = jnp.dot(hn, wg_ref[...], preferred_element_type=jnp.float32)
    b = jnp.dot(hn, wu_ref[...], preferred_element_type=jnp.float32)
    mid = (a * jax.nn.sigmoid(a) * b).astype(jnp.bfloat16)
    o_ref[...] = h + jnp.dot(mid, wd_ref[...], preferred_element_type=jnp.float32)


def _dense_ffn(h, g, wg, wu, wd):
    row = lambda i: (i, 0)
    return pl.pallas_call(
        _dense_ffn_kernel,
        out_shape=jax.ShapeDtypeStruct((L, D), jnp.float32),
        grid=(L // TM,),
        in_specs=[
            pl.BlockSpec((TM, D), row),
            _const_spec((1, D)),
            _const_spec((D, D_FF)),
            _const_spec((D, D_FF)),
            _const_spec((D_FF, D)),
        ],
        out_specs=pl.BlockSpec((TM, D), row),
        compiler_params=pltpu.CompilerParams(
            dimension_semantics=("arbitrary",), vmem_limit_bytes=VMEM_LIMIT),
        name="dense_ffn",
    )(h, g, wg, wu, wd)


def _pool_router_kernel(h_ref, gm_ref, wp_ref, sc_ref, gf_ref, wrh_ref, wrl_ref, o_ref, r_ref, pbuf):
    i = pl.program_id(0)

    @pl.when(i == 0)
    def _():
        pbuf[:, 0:POOL_HALO, :] = jnp.zeros((N_LB, POOL_HALO, LANES), jnp.float32)

    h = h_ref[...]
    hn = _rms(h, gm_ref[...])
    for j in range(N_LB):
        pbuf[j, POOL_HALO:POOL_HALO + TM, :] = hn[:, j * LANES:(j + 1) * LANES]
    pos = i * TM + lax.broadcasted_iota(jnp.int32, (TM, 1), 0)
    lb_per_group = POOL_GROUP // LANES
    mixed = []
    for gi, w in enumerate(POOL_WINDOWS):
        inv = 1.0 / jnp.minimum(pos + 1, w).astype(jnp.float32)
        cols = []
        for j in range(gi * lb_per_group, (gi + 1) * lb_per_group):
            x = hn[:, j * LANES:(j + 1) * LANES]
            s = x
            for d in range(1, w):
                s = s + pbuf[j, POOL_HALO - d:POOL_HALO - d + TM, :]
            cols.append(s * inv - x)
        pooled = jnp.concatenate(cols, axis=1).astype(jnp.bfloat16)
        mixed.append(jnp.dot(pooled, wp_ref[gi], preferred_element_type=jnp.float32))
    for j in range(N_LB):
        pbuf[j, 0:POOL_HALO, :] = pbuf[j, TM:TM + POOL_HALO, :]
    h = h + jnp.concatenate(mixed, axis=-1) * sc_ref[...]
    _to_token_tiles(o_ref, 0, h)

    hn2 = _rms(h, gf_ref[...])
    xh = hn2.astype(jnp.bfloat16)
    xl = (hn2 - xh.astype(jnp.float32)).astype(jnp.bfloat16)
    logits = (jnp.dot(xh, wrh_ref[...], preferred_element_type=jnp.float32)
              + jnp.dot(xl, wrh_ref[...], preferred_element_type=jnp.float32)
              + jnp.dot(xh, wrl_ref[...], preferred_element_type=jnp.float32))
    lane = lax.broadcasted_iota(jnp.int32, (TM, ROUTE_LANES), 1)
    neg = jnp.float32(-jnp.inf)
    lg = jnp.where(lane < N_EXPERTS, logits, neg)
    m1 = jnp.max(lg, axis=-1, keepdims=True)
    i1 = jnp.min(jnp.where(lg == m1, lane, ROUTE_LANES), axis=-1, keepdims=True)
    lg2 = jnp.where(lane == i1, neg, lg)
    m2 = jnp.max(lg2, axis=-1, keepdims=True)
    i2 = jnp.min(jnp.where(lg2 == m2, lane, ROUTE_LANES), axis=-1, keepdims=True)
    ex = jnp.exp(m2 - m1)
    w1 = 1.0 / (1.0 + ex)
    w2 = ex / (1.0 + ex)
    r = jnp.where(lane == 0, w1, 0.0)
    r = jnp.where(lane == 1, w2, r)
    r = jnp.where(lane == 2, i1.astype(jnp.float32), r)
    r = jnp.where(lane == 3, i2.astype(jnp.float32), r)
    r_ref[...] = r


def _pool_router(h, gm, wp, sc, gf, wrh, wrl):
    row = lambda i: (i, 0)
    return pl.pallas_call(
        _pool_router_kernel,
        out_shape=(jax.ShapeDtypeStruct((L * SUBLANES, LANES), jnp.float32),
                   jax.ShapeDtypeStruct((L, ROUTE_LANES), jnp.float32)),
        grid=(L // TM,),
        in_specs=[
            pl.BlockSpec((TM, D), row),
            _const_spec((1, D)),
            _const_spec((len(POOL_WINDOWS), POOL_GROUP, POOL_GROUP)),
            _const_spec((1, D)),
            _const_spec((1, D)),
            _const_spec((D, ROUTE_LANES)),
            _const_spec((D, ROUTE_LANES)),
        ],
        out_specs=(pl.BlockSpec((TM * SUBLANES, LANES), row), pl.BlockSpec((TM, ROUTE_LANES), row)),
        scratch_shapes=[pltpu.VMEM((N_LB, POOL_HALO + TM, LANES), jnp.float32)],
        compiler_params=pltpu.CompilerParams(
            dimension_semantics=("arbitrary",), vmem_limit_bytes=VMEM_LIMIT),
        name="pool_router",
    )(h, gm, wp, sc, gf, wrh, wrl)


def _moe_kernel(gexp_ref, gstart_ref, grows_ref, gsrc_ref, sdst_ref,
                h_hbm, g_ref, wg_ref, wu_ref, wd_ref, y_hbm,
                xg, xb, acc, ys, wgb, wub, wdb, gsem, ssem):
    del gexp_ref
    g = pl.program_id(0)
    c = pl.program_id(1)
    rows = grows_ref[g]
    start = gstart_ref[g]
    nsub = (rows + SUB_ROWS - 1) // SUB_ROWS
    first = c == 0
    last = c == N_FF_CHUNKS - 1
    tile = lambda r: pl.ds(pl.multiple_of(r * SUBLANES, SUBLANES), SUBLANES)

    rows8 = lambda v: pl.ds(pl.multiple_of(v, SUBLANES), SUBLANES)

    def scatter_copy(st, j):
        return pltpu.make_async_copy(ys.at[tile(j)], y_hbm.at[rows8(sdst_ref[st + j])], ssem)

    def block_wait(sem):
        blk = pl.ds(0, DMA_UNROLL * SUBLANES)
        pltpu.make_async_copy(h_hbm.at[blk], xg.at[blk], sem).wait()

    def gather_start(grp):
        st = gstart_ref[grp]
        nblk = (grows_ref[grp] + DMA_UNROLL - 1) // DMA_UNROLL

        def body(b, carry):
            base = st + b * DMA_UNROLL
            nrow = DMA_UNROLL * SUBLANES
            dst = xg.at[pl.ds(pl.multiple_of(b * nrow, nrow), nrow)]
            for u in range(DMA_UNROLL):
                pltpu.make_async_copy(h_hbm.at[rows8(gsrc_ref[base + u])],
                                      dst.at[pl.ds(u * SUBLANES, SUBLANES)], gsem).start()
            return carry
        lax.fori_loop(0, nblk, body, 0)

    def gather_wait(grp):
        nblk = (grows_ref[grp] + DMA_UNROLL - 1) // DMA_UNROLL

        def body(b, carry):
            block_wait(gsem)
            return carry
        lax.fori_loop(0, nblk, body, 0)

    def scatter_start(grp):
        st = gstart_ref[grp]
        n = grows_ref[grp]
        nblk = n // DMA_UNROLL

        def body(b, carry):
            for u in range(DMA_UNROLL):
                scatter_copy(st, b * DMA_UNROLL + u).start()
            return carry
        lax.fori_loop(0, nblk, body, 0)

        def tail(j, carry):
            scatter_copy(st, j).start()
            return carry
        lax.fori_loop(nblk * DMA_UNROLL, n, tail, 0)

    def scatter_wait(grp):
        n = grows_ref[grp]
        nblk = n // DMA_UNROLL

        def body(b, carry):
            block_wait(ssem)
            return carry
        lax.fori_loop(0, nblk, body, 0)

        def tail(j, carry):
            scatter_copy(0, 0).wait()
            return carry
        lax.fori_loop(nblk * DMA_UNROLL, n, tail, 0)

    gprev = jnp.maximum(g - 1, 0)
    gnext = jnp.minimum(g + 1, N_GROUPS - 1)
    prev_live = (g > 0) & (grows_ref[gprev] > 0)

    @pl.when(first & (g == 0) & (rows > 0))
    def _():
        xg[...] = jnp.zeros((GROUP_ROWS * SUBLANES, LANES), jnp.float32)
        gather_start(0)

    @pl.when(first & (rows > 0))
    def _():
        gather_wait(g)

        def norm(s, carry):
            r0 = pl.multiple_of(s * SUB_ROWS, SUB_ROWS)
            x = _from_token_tiles(xg, r0, SUB_ROWS)
            xb[pl.ds(r0, SUB_ROWS), :] = _rms(x, g_ref[...]).astype(jnp.bfloat16)
            return carry
        lax.fori_loop(0, nsub, norm, 0)

        @pl.when((g + 1 < N_GROUPS) & (grows_ref[gnext] > 0))
        def _():
            gather_start(gnext)

    def expert_out(s):
        r0 = pl.multiple_of(s * SUB_ROWS, SUB_ROWS)
        x = xb[pl.ds(r0, SUB_ROWS), :]
        a = jnp.dot(x, wgb[...], preferred_element_type=jnp.float32)
        b = jnp.dot(x, wub[...], preferred_element_type=jnp.float32)
        mid = (a * jax.nn.sigmoid(a) * b).astype(jnp.bfloat16)
        return r0, jnp.dot(mid, wdb[...], preferred_element_type=jnp.float32)

    @pl.when(rows > 0)
    def _():
        wgb[...] = wg_ref[0].astype(jnp.bfloat16)
        wub[...] = wu_ref[0].astype(jnp.bfloat16)
        wdb[...] = wd_ref[0].astype(jnp.bfloat16)

    @pl.when(first & (rows > 0))
    def _():
        def sub(s, carry):
            r0, y = expert_out(s)
            acc[pl.ds(r0, SUB_ROWS), :] = y
            return carry
        lax.fori_loop(0, nsub, sub, 0)

    @pl.when(jnp.logical_not(first) & jnp.logical_not(last) & (rows > 0))
    def _():
        def sub(s, carry):
            r0, y = expert_out(s)
            acc[pl.ds(r0, SUB_ROWS), :] += y
            return carry
        lax.fori_loop(0, nsub, sub, 0)

    @pl.when(last & prev_live)
    def _():
        scatter_wait(gprev)

    @pl.when(last & (rows > 0))
    def _():
        def sub(s, carry):
            r0, y = expert_out(s)
            _to_token_tiles(ys, r0, acc[pl.ds(r0, SUB_ROWS), :] + y)
            return carry
        lax.fori_loop(0, nsub, sub, 0)
        scatter_start(g)

        @pl.when(g == N_GROUPS - 1)
        def _():
            scatter_wait(g)


def _moe(gexp, gstart, grows, gsrc, sdst, h, g, wg, wu, wd):
    def w_in(gi, ci, gexp, gstart, grows, *_):
        return (gexp[gi], 0, jnp.where(grows[gi] > 0, ci, N_FF_CHUNKS - 1))

    def w_out(gi, ci, gexp, gstart, grows, *_):
        return (gexp[gi], jnp.where(grows[gi] > 0, ci, N_FF_CHUNKS - 1), 0)

    grid_spec = pltpu.PrefetchScalarGridSpec(
        num_scalar_prefetch=5,
        grid=(N_GROUPS, N_FF_CHUNKS),
        in_specs=[
            pl.BlockSpec(memory_space=pl.ANY),
            pl.BlockSpec((1, D), lambda gi, ci, *_: (0, 0)),
            pl.BlockSpec((1, D, FF_CHUNK), w_in),
            pl.BlockSpec((1, D, FF_CHUNK), w_in),
            pl.BlockSpec((1, FF_CHUNK, D), w_out),
        ],
        out_specs=pl.BlockSpec(memory_space=pl.ANY),
        scratch_shapes=[
            pltpu.VMEM((GROUP_ROWS * SUBLANES, LANES), jnp.float32),
            pltpu.VMEM((GROUP_ROWS, D), jnp.bfloat16),
            pltpu.VMEM((GROUP_ROWS, D), jnp.float32),
            pltpu.VMEM((GROUP_ROWS * SUBLANES, LANES), jnp.float32),
            pltpu.VMEM((D, FF_CHUNK), jnp.bfloat16),
            pltpu.VMEM((D, FF_CHUNK), jnp.bfloat16),
            pltpu.VMEM((FF_CHUNK, D), jnp.bfloat16),
            pltpu.SemaphoreType.DMA(()),
            pltpu.SemaphoreType.DMA(()),
        ],
    )
    return pl.pallas_call(
        _moe_kernel,
        out_shape=jax.ShapeDtypeStruct((N_ASSIGN * SUBLANES, LANES), jnp.float32),
        grid_spec=grid_spec,
        compiler_params=pltpu.CompilerParams(
            dimension_semantics=("arbitrary", "arbitrary"), vmem_limit_bytes=VMEM_LIMIT),
        name="moe_experts",
    )(gexp, gstart, grows, gsrc, sdst, h, g, wg, wu, wd)


def _final_kernel(h_ref, y0_ref, y1_ref, r_ref, g_ref, o_ref):
    r = r_ref[...]
    h = (_from_token_tiles(h_ref, 0, TF)
         + r[:, 0:1] * _from_token_tiles(y0_ref, 0, TF)
         + r[:, 1:2] * _from_token_tiles(y1_ref, 0, TF))
    o_ref[...] = _rms(h, g_ref[...])


def _final(h, y, r, g):
    return pl.pallas_call(
        _final_kernel,
        out_shape=jax.ShapeDtypeStruct((SEQ, D), jnp.float32),
        grid=(SEQ // TF,),
        in_specs=[
            pl.BlockSpec((pl.Element(TF * SUBLANES), pl.Element(LANES)),
                         lambda i: ((N_META + i * TF) * SUBLANES, 0)),
            pl.BlockSpec((TF * SUBLANES, LANES), lambda i: (i, 0)),
            pl.BlockSpec((TF * SUBLANES, LANES), lambda i: (i + SEQ // TF, 0)),
            pl.BlockSpec((pl.Element(TF), pl.Element(ROUTE_LANES)),
                         lambda i: (SUBLANES * (N_META // SUBLANES + i * (TF // SUBLANES)), 0)),
            _const_spec((1, D)),
        ],
        out_specs=pl.BlockSpec((TF, D), lambda i: (i, 0)),
        compiler_params=pltpu.CompilerParams(
            dimension_semantics=("arbitrary",), vmem_limit_bytes=VMEM_LIMIT),
        name="combine_norm",
    )(h, y, y, r, g)


def _group_table(flat_e):
    order = jnp.argsort(flat_e, stable=True).astype(jnp.int32)
    order = jnp.concatenate([order, jnp.zeros((DMA_UNROLL,), jnp.int32)])
    gsrc = (N_META + (order & (SEQ - 1))) * SUBLANES
    sdst = order * SUBLANES
    experts = jnp.arange(N_EXPERTS, dtype=jnp.int32)
    counts = jnp.sum(flat_e[:, None] == experts[None, :], axis=0, dtype=jnp.int32)
    ngrp = (counts + GROUP_ROWS - 1) // GROUP_ROWS
    cum_grp = jnp.cumsum(ngrp)
    total = cum_grp[-1]
    first_row = jnp.cumsum(counts) - counts
    gi = jnp.arange(N_GROUPS, dtype=jnp.int32)
    valid = gi < total
    gclip = jnp.minimum(gi, total - 1)
    e = jnp.sum(gclip[:, None] >= cum_grp[None, :], axis=1, dtype=jnp.int32)
    e = jnp.clip(e, 0, N_EXPERTS - 1)
    local = gclip - (cum_grp[e] - ngrp[e])
    gstart = jnp.where(valid, first_row[e] + local * GROUP_ROWS, 0).astype(jnp.int32)
    grows = jnp.where(valid, jnp.clip(counts[e] - local * GROUP_ROWS, 0, GROUP_ROWS), 0).astype(jnp.int32)
    return gsrc, sdst, e, gstart, grows


def kernel(x, meta_tokens, conv_w_pw1, conv_b_pw1, conv_w_dw, conv_b_dw, conv_ln_g, conv_ln_b,
           conv_w_pw2, conv_b_pw2, pool_w_group, pool_scale, ffn_w_gate, ffn_w_up, ffn_w_down,
           moe_w_router, moe_w_gate, moe_w_up, moe_w_down, mix_norm_g, ffn_norm_g, final_norm_g):
    bf = jnp.bfloat16
    h = _conv_mixer(x[0], meta_tokens.astype(x.dtype), mix_norm_g[0:1], conv_w_pw1[0].astype(bf),
                    conv_b_pw1, conv_w_dw[0], conv_b_dw, conv_ln_g, conv_ln_b,
                    conv_w_pw2[0].astype(bf), conv_b_pw2)
    h = _dense_ffn(h, ffn_norm_g[0:1], ffn_w_gate[0].astype(bf), ffn_w_up[0].astype(bf),
                   ffn_w_down[0].astype(bf))

    wr = jnp.pad(moe_w_router[0], ((0, 0), (0, ROUTE_LANES - N_EXPERTS)))
    wrh = wr.astype(bf)
    wrl = (wr - wrh.astype(jnp.float32)).astype(bf)
    ht, route = _pool_router(h, mix_norm_g[1:2], pool_w_group[0].astype(bf), pool_scale,
                             ffn_norm_g[1:2], wrh, wrl)

    flat_e = jnp.concatenate([route[N_META:, 2], route[N_META:, 3]]).astype(jnp.int32)
    gsrc, sdst, gexp, gstart, grows = _group_table(flat_e)
    y = _moe(gexp, gstart, grows, gsrc, sdst, ht, ffn_norm_g[1:2], moe_w_gate[0], moe_w_up[0], moe_w_down[0])

    out = _final(ht, y, route, final_norm_g[None, :])
    return out[None]
```

```python
import jax
import jax.numpy as jnp
from jax import lax
from jax.experimental import pallas as pl
from jax.experimental.pallas import tpu as pltpu

D = 1024
SEQ = 16384
N_META = 16
L = SEQ + N_META
CONV_WIDTH = 31
POOL_WINDOWS = (2, 4, 8, 16)
POOL_GROUP = D // len(POOL_WINDOWS)
D_FF = 2816
N_EXPERTS = 8
D_FF_EXPERT = 3584
RMS_EPS = 1e-6
LN_EPS = 1e-5

LANES = 128
SUBLANES = 8
N_LB = D // LANES
TM = 656
CONV_HALO = 32
CONV_ROWS = 16
CONV_PARTS = 4
POOL_HALO = 16
ROUTE_LANES = LANES
GROUP_ROWS = 2560
SUB_ROWS = 640
N_SUB = GROUP_ROWS // SUB_ROWS
FF_CHUNK = 512
N_FF_CHUNKS = D_FF_EXPERT // FF_CHUNK
N_ASSIGN = 2 * SEQ
N_GROUPS = -(-N_ASSIGN // GROUP_ROWS) + N_EXPERTS
DMA_UNROLL = 8
TF = 1024
VMEM_LIMIT = 56 * 1024 * 1024


def _rms(x, g):
    return x * lax.rsqrt(jnp.mean(x * x, axis=-1, keepdims=True) + RMS_EPS) * g


def _const_spec(shape):
    zeros = (0,) * len(shape)
    return pl.BlockSpec(shape, lambda *_: zeros, pipeline_mode=pl.Buffered(1))


def _to_token_tiles(ref, row0, x):
    n = x.shape[0]
    for j in range(N_LB):
        ref[pl.ds(row0 * SUBLANES + j, n, stride=SUBLANES), :] = x[:, j * LANES:(j + 1) * LANES]


def _from_token_tiles(ref, row0, n):
    return jnp.concatenate(
        [ref[pl.ds(row0 * SUBLANES + j, n, stride=SUBLANES), :] for j in range(N_LB)], axis=1)


def _conv_mixer_kernel(x_ref, meta_ref, g_ref, w1_ref, b1_ref, wdw_ref, bdw_ref, lng_ref, lnb_ref,
                       w2_ref, b2_ref, o_ref, hbuf, ubuf, cbuf, wbc):
    i = pl.program_id(0)

    @pl.when(i == 0)
    def _():
        ubuf[:, 0:CONV_HALO, :] = jnp.zeros((N_LB, CONV_HALO, LANES), jnp.float32)
        for k in range(CONV_WIDTH):
            wbc[k * SUBLANES:(k + 1) * SUBLANES, :] = jnp.broadcast_to(wdw_ref[k:k + 1, :], (SUBLANES, D))
        hbuf[0:N_META, :] = meta_ref[...]
        hbuf[N_META:TM, :] = x_ref[0:TM - N_META, :]

    @pl.when(i > 0)
    def _():
        hbuf[...] = x_ref[...]

    h = hbuf[...]
    hn = _rms(h, g_ref[...]).astype(jnp.bfloat16)
    u = jnp.dot(hn, w1_ref[...], preferred_element_type=jnp.float32) + b1_ref[...]
    u = u[:, :D] * jax.nn.sigmoid(u[:, D:])
    for j in range(N_LB):
        ubuf[j, CONV_HALO:CONV_HALO + TM, :] = u[:, j * LANES:(j + 1) * LANES]

    for j in range(N_LB):
        lanes = slice(j * LANES, (j + 1) * LANES)
        taps = [wbc[k * SUBLANES:(k + 1) * SUBLANES, lanes] for k in range(CONV_WIDTH)]
        bias = jnp.broadcast_to(bdw_ref[:, lanes], (SUBLANES, LANES))

        def chunk(r, carry, j=j, lanes=lanes, taps=taps, bias=bias):
            base = pl.multiple_of(r * CONV_ROWS, CONV_ROWS)
            for q in range(CONV_ROWS // SUBLANES):
                parts = [None] * CONV_PARTS
                for k in range(CONV_WIDTH):
                    d = CONV_WIDTH - 1 - k
                    term = ubuf[j, pl.ds(CONV_HALO + base + q * SUBLANES - d, SUBLANES), :] * taps[k]
                    parts[k % CONV_PARTS] = term if parts[k % CONV_PARTS] is None else parts[k % CONV_PARTS] + term
                cbuf[pl.ds(base + q * SUBLANES, SUBLANES), lanes] = (
                    (parts[0] + parts[1]) + (parts[2] + parts[3]) + bias)
            return carry

        lax.fori_loop(0, TM // CONV_ROWS, chunk, 0)
        ubuf[j, 0:CONV_HALO, :] = ubuf[j, TM:TM + CONV_HALO, :]

    c = cbuf[...]
    mu = jnp.mean(c, axis=-1, keepdims=True)
    cc = c - mu
    var = jnp.mean(cc * cc, axis=-1, keepdims=True)
    y = cc * lax.rsqrt(var + LN_EPS) * lng_ref[...] + lnb_ref[...]
    y = (y * jax.nn.sigmoid(y)).astype(jnp.bfloat16)
    o_ref[...] = h + jnp.dot(y, w2_ref[...], preferred_element_type=jnp.float32) + b2_ref[...]


def _conv_mixer(x, meta, g, w1, b1, wdw, bdw, lng, lnb, w2, b2):
    row = lambda i: (i, 0)
    return pl.pallas_call(
        _conv_mixer_kernel,
        out_shape=jax.ShapeDtypeStruct((L, D), jnp.float32),
        grid=(L // TM,),
        in_specs=[
            pl.BlockSpec((pl.Element(TM), pl.Element(D)),
                         lambda i: (SUBLANES * jnp.maximum(i * (TM // SUBLANES) - N_META // SUBLANES, 0), 0)),
            _const_spec((N_META, D)),
            _const_spec((1, D)),
            _const_spec((D, 2 * D)),
            _const_spec((1, 2 * D)),
            _const_spec((CONV_WIDTH, D)),
            _const_spec((1, D)),
            _const_spec((1, D)),
            _const_spec((1, D)),
            _const_spec((D, D)),
            _const_spec((1, D)),
        ],
        out_specs=pl.BlockSpec((TM, D), row),
        scratch_shapes=[
            pltpu.VMEM((TM, D), jnp.float32),
            pltpu.VMEM((N_LB, CONV_HALO + TM, LANES), jnp.float32),
            pltpu.VMEM((TM, D), jnp.float32),
            pltpu.VMEM((CONV_WIDTH * SUBLANES, D), jnp.float32),
        ],
        compiler_params=pltpu.CompilerParams(
            dimension_semantics=("arbitrary",), vmem_limit_bytes=VMEM_LIMIT),
        name="conv_mixer",
    )(x, meta, g, w1, b1, wdw, bdw, lng, lnb, w2, b2)


def _dense_ffn_kernel(h_ref, g_ref, wg_ref, wu_ref, wd_ref, o_ref):
    h = h_ref[...]
    hn = _rms(h, g_ref[...]).astype(jnp.bfloat16)
    a = jnp.dot(hn, wg_ref[...], preferred_element_type=jnp.float32)
    b = jnp.dot(hn, wu_ref[...], preferred_element_type=jnp.float32)
    mid = (a * jax.nn.sigmoid(a) * b).astype(jnp.bfloat16)
    o_ref[...] = h + jnp.dot(mid, wd_ref[...], preferred_element_type=jnp.float32)


def _dense_ffn(h, g, wg, wu, wd):
    row = lambda i: (i, 0)
    return pl.pallas_call(
        _dense_ffn_kernel,
        out_shape=jax.ShapeDtypeStruct((L, D), jnp.float32),
        grid=(L // TM,),
        in_specs=[
            pl.BlockSpec((TM, D), row),
            _const_spec((1, D)),
            _const_spec((D, D_FF)),
            _const_spec((D, D_FF)),
            _const_spec((D_FF, D)),
        ],
        out_specs=pl.BlockSpec((TM, D), row),
        compiler_params=pltpu.CompilerParams(
            dimension_semantics=("arbitrary",), vmem_limit_bytes=VMEM_LIMIT),
        name="dense_ffn",
    )(h, g, wg, wu, wd)


def _pool_router_kernel(h_ref, gm_ref, wp_ref, sc_ref, gf_ref, wrh_ref, wrl_ref, o_ref, r_ref, pbuf):
    i = pl.program_id(0)

    @pl.when(i == 0)
    def _():
        pbuf[:, 0:POOL_HALO, :] = jnp.zeros((N_LB, POOL_HALO, LANES), jnp.float32)

    h = h_ref[...]
    hn = _rms(h, gm_ref[...])
    for j in range(N_LB):
        pbuf[j, POOL_HALO:POOL_HALO + TM, :] = hn[:, j * LANES:(j + 1) * LANES]
    pos = i * TM + lax.broadcasted_iota(jnp.int32, (TM, 1), 0)
    lb_per_group = POOL_GROUP // LANES
    mixed = []
    for gi, w in enumerate(POOL_WINDOWS):
        inv = 1.0 / jnp.minimum(pos + 1, w).astype(jnp.float32)
        cols = []
        for j in range(gi * lb_per_group, (gi + 1) * lb_per_group):
            x = hn[:, j * LANES:(j + 1) * LANES]
            s = x
            for d in range(1, w):
                s = s + pbuf[j, POOL_HALO - d:POOL_HALO - d + TM, :]
            cols.append(s * inv - x)
        pooled = jnp.concatenate(cols, axis=1).astype(jnp.bfloat16)
        mixed.append(jnp.dot(pooled, wp_ref[gi], preferred_element_type=jnp.float32))
    for j in range(N_LB):
        pbuf[j, 0:POOL_HALO, :] = pbuf[j, TM:TM + POOL_HALO, :]
    h = h + jnp.concatenate(mixed, axis=-1) * sc_ref[...]
    _to_token_tiles(o_ref, 0, h)

    hn2 = _rms(h, gf_ref[...])
    xh = hn2.astype(jnp.bfloat16)
    xl = (hn2 - xh.astype(jnp.float32)).astype(jnp.bfloat16)
    logits = (jnp.dot(xh, wrh_ref[...], preferred_element_type=jnp.float32)
              + jnp.dot(xl, wrh_ref[...], preferred_element_type=jnp.float32)
              + jnp.dot(xh, wrl_ref[...], preferred_element_type=jnp.float32))
    lane = lax.broadcasted_iota(jnp.int32, (TM, ROUTE_LANES), 1)
    neg = jnp.float32(-jnp.inf)
    lg = jnp.where(lane < N_EXPERTS, logits, neg)
    m1 = jnp.max(lg, axis=-1, keepdims=True)
    i1 = jnp.min(jnp.where(lg == m1, lane, ROUTE_LANES), axis=-1, keepdims=True)
    lg2 = jnp.where(lane == i1, neg, lg)
    m2 = jnp.max(lg2, axis=-1, keepdims=True)
    i2 = jnp.min(jnp.where(lg2 == m2, lane, ROUTE_LANES), axis=-1, keepdims=True)
    ex = jnp.exp(m2 - m1)
    w1 = 1.0 / (1.0 + ex)
    w2 = ex / (1.0 + ex)
    r = jnp.where(lane == 0, w1, 0.0)
    r = jnp.where(lane == 1, w2, r)
    r = jnp.where(lane == 2, i1.astype(jnp.float32), r)
    r = jnp.where(lane == 3, i2.astype(jnp.float32), r)
    r_ref[...] = r


def _pool_router(h, gm, wp, sc, gf, wrh, wrl):
    row = lambda i: (i, 0)
    return pl.pallas_call(
        _pool_router_kernel,
        out_shape=(jax.ShapeDtypeStruct((L * SUBLANES, LANES), jnp.float32),
                   jax.ShapeDtypeStruct((L, ROUTE_LANES), jnp.float32)),
        grid=(L // TM,),
        in_specs=[
            pl.BlockSpec((TM, D), row),
            _const_spec((1, D)),
            _const_spec((len(POOL_WINDOWS), POOL_GROUP, POOL_GROUP)),
            _const_spec((1, D)),
            _const_spec((1, D)),
            _const_spec((D, ROUTE_LANES)),
            _const_spec((D, ROUTE_LANES)),
        ],
        out_specs=(pl.BlockSpec((TM * SUBLANES, LANES), row), pl.BlockSpec((TM, ROUTE_LANES), row)),
        scratch_shapes=[pltpu.VMEM((N_LB, POOL_HALO + TM, LANES), jnp.float32)],
        compiler_params=pltpu.CompilerParams(
            dimension_semantics=("arbitrary",), vmem_limit_bytes=VMEM_LIMIT),
        name="pool_router",
    )(h, gm, wp, sc, gf, wrh, wrl)


def _moe_kernel(gexp_ref, gstart_ref, grows_ref, gsrc_ref, sdst_ref,
                h_hbm, g_ref, wg_ref, wu_ref, wd_ref, y_hbm,
                xg, xb, acc, ys, wgb, wub, wdb, gsem, ssem):
    del gexp_ref
    g = pl.program_id(0)
    c = pl.program_id(1)
    rows = grows_ref[g]
    start = gstart_ref[g]
    nsub = (rows + SUB_ROWS - 1) // SUB_ROWS
    first = c == 0
    last = c == N_FF_CHUNKS - 1
    tile = lambda r: pl.ds(pl.multiple_of(r * SUBLANES, SUBLANES), SUBLANES)

    rows8 = lambda v: pl.ds(pl.multiple_of(v, SUBLANES), SUBLANES)

    def scatter_copy(st, j):
        return pltpu.make_async_copy(ys.at[tile(j)], y_hbm.at[rows8(sdst_ref[st + j])], ssem)

    def block_wait(sem):
        blk = pl.ds(0, DMA_UNROLL * SUBLANES)
        pltpu.make_async_copy(h_hbm.at[blk], xg.at[blk], sem).wait()

    def gather_start(grp):
        st = gstart_ref[grp]
        nblk = (grows_ref[grp] + DMA_UNROLL - 1) // DMA_UNROLL

        def body(b, carry):
            base = st + b * DMA_UNROLL
            nrow = DMA_UNROLL * SUBLANES
            dst = xg.at[pl.ds(pl.multiple_of(b * nrow, nrow), nrow)]
            for u in range(DMA_UNROLL):
                pltpu.make_async_copy(h_hbm.at[rows8(gsrc_ref[base + u])],
                                      dst.at[pl.ds(u * SUBLANES, SUBLANES)], gsem).start(priority=u % 2)
            return carry
        lax.fori_loop(0, nblk, body, 0)

    def gather_wait(grp):
        nblk = (grows_ref[grp] + DMA_UNROLL - 1) // DMA_UNROLL

        def body(b, carry):
            block_wait(gsem)
            return carry
        lax.fori_loop(0, nblk, body, 0)

    def scatter_start(grp):
        st = gstart_ref[grp]
        n = grows_ref[grp]
        nblk = n // DMA_UNROLL

        def body(b, carry):
            for u in range(DMA_UNROLL):
                scatter_copy(st, b * DMA_UNROLL + u).start(priority=u % 2)
            return carry
        lax.fori_loop(0, nblk, body, 0)

        def tail(j, carry):
            scatter_copy(st, j).start()
            return carry
        lax.fori_loop(nblk * DMA_UNROLL, n, tail, 0)

    def scatter_wait(grp):
        n = grows_ref[grp]
        nblk = n // DMA_UNROLL

        def body(b, carry):
            block_wait(ssem)
            return carry
        lax.fori_loop(0, nblk, body, 0)

        def tail(j, carry):
            scatter_copy(0, 0).wait()
            return carry
        lax.fori_loop(nblk * DMA_UNROLL, n, tail, 0)

    gprev = jnp.maximum(g - 1, 0)
    gnext = jnp.minimum(g + 1, N_GROUPS - 1)
    prev_live = (g > 0) & (grows_ref[gprev] > 0)

    @pl.when(first & (g == 0) & (rows > 0))
    def _():
        xg[...] = jnp.zeros((GROUP_ROWS * SUBLANES, LANES), jnp.float32)
        gather_start(0)

    @pl.when(first & (rows > 0))
    def _():
        gather_wait(g)

        def norm(s, carry):
            r0 = pl.multiple_of(s * SUB_ROWS, SUB_ROWS)
            x = _from_token_tiles(xg, r0, SUB_ROWS)
            xb[pl.ds(r0, SUB_ROWS), :] = _rms(x, g_ref[...]).astype(jnp.bfloat16)
            return carry
        lax.fori_loop(0, nsub, norm, 0)

        @pl.when((g + 1 < N_GROUPS) & (grows_ref[gnext] > 0))
        def _():
            gather_start(gnext)

    def expert_out(s):
        r0 = pl.multiple_of(s * SUB_ROWS, SUB_ROWS)
        x = xb[pl.ds(r0, SUB_ROWS), :]
        a = jnp.dot(x, wgb[...], preferred_element_type=jnp.float32)
        b = jnp.dot(x, wub[...], preferred_element_type=jnp.float32)
        mid = (a * jax.nn.sigmoid(a) * b).astype(jnp.bfloat16)
        return r0, jnp.dot(mid, wdb[...], preferred_element_type=jnp.float32)

    @pl.when(rows > 0)
    def _():
        wgb[...] = wg_ref[0].astype(jnp.bfloat16)
        wub[...] = wu_ref[0].astype(jnp.bfloat16)
        wdb[...] = wd_ref[0].astype(jnp.bfloat16)

    @pl.when(first & (rows > 0))
    def _():
        def sub(s, carry):
            r0, y = expert_out(s)
            acc[pl.ds(r0, SUB_ROWS), :] = y
            return carry
        lax.fori_loop(0, nsub, sub, 0)

    @pl.when(jnp.logical_not(first) & jnp.logical_not(last) & (rows > 0))
    def _():
        def sub(s, carry):
            r0, y = expert_out(s)
            acc[pl.ds(r0, SUB_ROWS), :] += y
            return carry
        lax.fori_loop(0, nsub, sub, 0)

    @pl.when(last & prev_live)
    def _():
        scatter_wait(gprev)

    @pl.when(last & (rows > 0))
    def _():
        def sub(s, carry):
            r0, y = expert_out(s)
            _to_token_tiles(ys, r0, acc[pl.ds(r0, SUB_ROWS), :] + y)
            return carry
        lax.fori_loop(0, nsub, sub, 0)
        scatter_start(g)

        @pl.when(g == N_GROUPS - 1)
        def _():
            scatter_wait(g)


def _moe(gexp, gstart, grows, gsrc, sdst, h, g, wg, wu, wd):
    def w_in(gi, ci, gexp, gstart, grows, *_):
        return (gexp[gi], 0, jnp.where(grows[gi] > 0, ci, N_FF_CHUNKS - 1))

    def w_out(gi, ci, gexp, gstart, grows, *_):
        return (gexp[gi], jnp.where(grows[gi] > 0, ci, N_FF_CHUNKS - 1), 0)

    grid_spec = pltpu.PrefetchScalarGridSpec(
        num_scalar_prefetch=5,
        grid=(N_GROUPS, N_FF_CHUNKS),
        in_specs=[
            pl.BlockSpec(memory_space=pl.ANY),
            pl.BlockSpec((1, D), lambda gi, ci, *_: (0, 0)),
            pl.BlockSpec((1, D, FF_CHUNK), w_in),
            pl.BlockSpec((1, D, FF_CHUNK), w_in),
            pl.BlockSpec((1, FF_CHUNK, D), w_out),
        ],
        out_specs=pl.BlockSpec(memory_space=pl.ANY),
        scratch_shapes=[
            pltpu.VMEM((GROUP_ROWS * SUBLANES, LANES), jnp.float32),
            pltpu.VMEM((GROUP_ROWS, D), jnp.bfloat16),
            pltpu.VMEM((GROUP_ROWS, D), jnp.float32),
            pltpu.VMEM((GROUP_ROWS * SUBLANES, LANES), jnp.float32),
            pltpu.VMEM((D, FF_CHUNK), jnp.bfloat16),
            pltpu.VMEM((D, FF_CHUNK), jnp.bfloat16),
            pltpu.VMEM((FF_CHUNK, D), jnp.bfloat16),
            pltpu.SemaphoreType.DMA(()),
            pltpu.SemaphoreType.DMA(()),
        ],
    )
    return pl.pallas_call(
        _moe_kernel,
        out_shape=jax.ShapeDtypeStruct((N_ASSIGN * SUBLANES, LANES), jnp.float32),
        grid_spec=grid_spec,
        compiler_params=pltpu.CompilerParams(
            dimension_semantics=("arbitrary", "arbitrary"), vmem_limit_bytes=VMEM_LIMIT),
        name="moe_experts",
    )(gexp, gstart, grows, gsrc, sdst, h, g, wg, wu, wd)


def _final_kernel(h_ref, y0_ref, y1_ref, r_ref, g_ref, o_ref):
    r = r_ref[...]
    h = (_from_token_tiles(h_ref, 0, TF)
         + r[:, 0:1] * _from_token_tiles(y0_ref, 0, TF)
         + r[:, 1:2] * _from_token_tiles(y1_ref, 0, TF))
    o_ref[...] = _rms(h, g_ref[...])


def _final(h, y, r, g):
    return pl.pallas_call(
        _final_kernel,
        out_shape=jax.ShapeDtypeStruct((SEQ, D), jnp.float32),
        grid=(SEQ // TF,),
        in_specs=[
            pl.BlockSpec((pl.Element(TF * SUBLANES), pl.Element(LANES)),
                         lambda i: ((N_META + i * TF) * SUBLANES, 0)),
            pl.BlockSpec((TF * SUBLANES, LANES), lambda i: (i, 0)),
            pl.BlockSpec((TF * SUBLANES, LANES), lambda i: (i + SEQ // TF, 0)),
            pl.BlockSpec((pl.Element(TF), pl.Element(ROUTE_LANES)),
                         lambda i: (SUBLANES * (N_META // SUBLANES + i * (TF // SUBLANES)), 0)),
            _const_spec((1, D)),
        ],
        out_specs=pl.BlockSpec((TF, D), lambda i: (i, 0)),
        compiler_params=pltpu.CompilerParams(
            dimension_semantics=("arbitrary",), vmem_limit_bytes=VMEM_LIMIT),
        name="combine_norm",
    )(h, y, y, r, g)


def _group_table(flat_e):
    order = jnp.argsort(flat_e, stable=True).astype(jnp.int32)
    order = jnp.concatenate([order, jnp.zeros((DMA_UNROLL,), jnp.int32)])
    gsrc = (N_META + (order & (SEQ - 1))) * SUBLANES
    sdst = order * SUBLANES
    experts = jnp.arange(N_EXPERTS, dtype=jnp.int32)
    counts = jnp.sum(flat_e[:, None] == experts[None, :], axis=0, dtype=jnp.int32)
    ngrp = (counts + GROUP_ROWS - 1) // GROUP_ROWS
    cum_grp = jnp.cumsum(ngrp)
    total = cum_grp[-1]
    first_row = jnp.cumsum(counts) - counts
    gi = jnp.arange(N_GROUPS, dtype=jnp.int32)
    valid = gi < total
    gclip = jnp.minimum(gi, total - 1)
    e = jnp.sum(gclip[:, None] >= cum_grp[None, :], axis=1, dtype=jnp.int32)
    e = jnp.clip(e, 0, N_EXPERTS - 1)
    local = gclip - (cum_grp[e] - ngrp[e])
    gstart = jnp.where(valid, first_row[e] + local * GROUP_ROWS, 0).astype(jnp.int32)
    grows = jnp.where(valid, jnp.clip(counts[e] - local * GROUP_ROWS, 0, GROUP_ROWS), 0).astype(jnp.int32)
    return gsrc, sdst, e, gstart, grows


def kernel(x, meta_tokens, conv_w_pw1, conv_b_pw1, conv_w_dw, conv_b_dw, conv_ln_g, conv_ln_b,
           conv_w_pw2, conv_b_pw2, pool_w_group, pool_scale, ffn_w_gate, ffn_w_up, ffn_w_down,
           moe_w_router, moe_w_gate, moe_w_up, moe_w_down, mix_norm_g, ffn_norm_g, final_norm_g):
    bf = jnp.bfloat16
    h = _conv_mixer(x[0], meta_tokens.astype(x.dtype), mix_norm_g[0:1], conv_w_pw1[0].astype(bf),
                    conv_b_pw1, conv_w_dw[0], conv_b_dw, conv_ln_g, conv_ln_b,
                    conv_w_pw2[0].astype(bf), conv_b_pw2)
    h = _dense_ffn(h, ffn_norm_g[0:1], ffn_w_gate[0].astype(bf), ffn_w_up[0].astype(bf),
                   ffn_w_down[0].astype(bf))

    wr = jnp.pad(moe_w_router[0], ((0, 0), (0, ROUTE_LANES - N_EXPERTS)))
    wrh = wr.astype(bf)
    wrl = (wr - wrh.astype(jnp.float32)).astype(bf)
    ht, route = _pool_router(h, mix_norm_g[1:2], pool_w_group[0].astype(bf), pool_scale,
                             ffn_norm_g[1:2], wrh, wrl)

    flat_e = jnp.concatenate([route[N_META:, 2], route[N_META:, 3]]).astype(jnp.int32)
    gsrc, sdst, gexp, gstart, grows = _group_table(flat_e)
    y = _moe(gexp, gstart, grows, gsrc, sdst, ht, ffn_norm_g[1:2], moe_w_gate[0], moe_w_up[0], moe_w_down[0])

    out = _final(ht, y, route, final_norm_g[None, :])
    return out[None]
```

```python
import jax
import jax.numpy as jnp
from jax import lax
from jax.experimental import pallas as pl
from jax.experimental.pallas import tpu as pltpu

D = 1024
SEQ = 16384
N_META = 16
L = SEQ + N_META
CONV_WIDTH = 31
POOL_WINDOWS = (2, 4, 8, 16)
POOL_GROUP = D // len(POOL_WINDOWS)
D_FF = 2816
N_EXPERTS = 8
D_FF_EXPERT = 3584
RMS_EPS = 1e-6
LN_EPS = 1e-5

LANES = 128
SUBLANES = 8
N_LB = D // LANES
TM = 656
CONV_HALO = 32
CONV_ROWS = 16
CONV_PARTS = 4
POOL_HALO = 16
ROUTE_LANES = LANES
GROUP_ROWS = 2560
SUB_ROWS = 640
N_SUB = GROUP_ROWS // SUB_ROWS
FF_CHUNK = 512
N_FF_CHUNKS = D_FF_EXPERT // FF_CHUNK
N_ASSIGN = 2 * SEQ
N_GROUPS = -(-N_ASSIGN // GROUP_ROWS) + N_EXPERTS
DMA_UNROLL = 8
WAIT_BLOCKS = 16
TF = 1024
VMEM_LIMIT = 56 * 1024 * 1024


def _rms(x, g):
    return x * lax.rsqrt(jnp.mean(x * x, axis=-1, keepdims=True) + RMS_EPS) * g


def _const_spec(shape):
    zeros = (0,) * len(shape)
    return pl.BlockSpec(shape, lambda *_: zeros, pipeline_mode=pl.Buffered(1))


def _to_token_tiles(ref, row0, x):
    n = x.shape[0]
    for j in range(N_LB):
        ref[pl.ds(row0 * SUBLANES + j, n, stride=SUBLANES), :] = x[:, j * LANES:(j + 1) * LANES]


def _from_token_tiles(ref, row0, n):
    return jnp.concatenate(
        [ref[pl.ds(row0 * SUBLANES + j, n, stride=SUBLANES), :] for j in range(N_LB)], axis=1)


def _conv_mixer_kernel(x_ref, meta_ref, g_ref, w1_ref, b1_ref, wdw_ref, bdw_ref, lng_ref, lnb_ref,
                       w2_ref, b2_ref, o_ref, hbuf, ubuf, cbuf, wbc):
    i = pl.program_id(0)

    @pl.when(i == 0)
    def _():
        ubuf[:, 0:CONV_HALO, :] = jnp.zeros((N_LB, CONV_HALO, LANES), jnp.float32)
        for k in range(CONV_WIDTH):
            wbc[k * SUBLANES:(k + 1) * SUBLANES, :] = jnp.broadcast_to(wdw_ref[k:k + 1, :], (SUBLANES, D))
        hbuf[0:N_META, :] = meta_ref[...]
        hbuf[N_META:TM, :] = x_ref[0:TM - N_META, :]

    @pl.when(i > 0)
    def _():
        hbuf[...] = x_ref[...]

    h = hbuf[...]
    hn = _rms(h, g_ref[...]).astype(jnp.bfloat16)
    u = jnp.dot(hn, w1_ref[...], preferred_element_type=jnp.float32) + b1_ref[...]
    u = u[:, :D] * jax.nn.sigmoid(u[:, D:])
    for j in range(N_LB):
        ubuf[j, CONV_HALO:CONV_HALO + TM, :] = u[:, j * LANES:(j + 1) * LANES]

    for j in range(N_LB):
        lanes = slice(j * LANES, (j + 1) * LANES)
        taps = [wbc[k * SUBLANES:(k + 1) * SUBLANES, lanes] for k in range(CONV_WIDTH)]
        bias = jnp.broadcast_to(bdw_ref[:, lanes], (SUBLANES, LANES))

        def chunk(r, carry, j=j, lanes=lanes, taps=taps, bias=bias):
            base = pl.multiple_of(r * CONV_ROWS, CONV_ROWS)
            for q in range(CONV_ROWS // SUBLANES):
                parts = [None] * CONV_PARTS
                for k in range(CONV_WIDTH):
                    d = CONV_WIDTH - 1 - k
                    term = ubuf[j, pl.ds(CONV_HALO + base + q * SUBLANES - d, SUBLANES), :] * taps[k]
                    parts[k % CONV_PARTS] = term if parts[k % CONV_PARTS] is None else parts[k % CONV_PARTS] + term
                cbuf[pl.ds(base + q * SUBLANES, SUBLANES), lanes] = (
                    (parts[0] + parts[1]) + (parts[2] + parts[3]) + bias)
            return carry

        lax.fori_loop(0, TM // CONV_ROWS, chunk, 0)
        ubuf[j, 0:CONV_HALO, :] = ubuf[j, TM:TM + CONV_HALO, :]

    c = cbuf[...]
    mu = jnp.mean(c, axis=-1, keepdims=True)
    cc = c - mu
    var = jnp.mean(cc * cc, axis=-1, keepdims=True)
    y = cc * lax.rsqrt(var + LN_EPS) * lng_ref[...] + lnb_ref[...]
    y = (y * jax.nn.sigmoid(y)).astype(jnp.bfloat16)
    o_ref[...] = h + jnp.dot(y, w2_ref[...], preferred_element_type=jnp.float32) + b2_ref[...]


def _conv_mixer(x, meta, g, w1, b1, wdw, bdw, lng, lnb, w2, b2):
    row = lambda i: (i, 0)
    return pl.pallas_call(
        _conv_mixer_kernel,
        out_shape=jax.ShapeDtypeStruct((L, D), jnp.float32),
        grid=(L // TM,),
        in_specs=[
            pl.BlockSpec((pl.Element(TM), pl.Element(D)),
                         lambda i: (SUBLANES * jnp.maximum(i * (TM // SUBLANES) - N_META // SUBLANES, 0), 0)),
            _const_spec((N_META, D)),
            _const_spec((1, D)),
            _const_spec((D, 2 * D)),
            _const_spec((1, 2 * D)),
            _const_spec((CONV_WIDTH, D)),
            _const_spec((1, D)),
            _const_spec((1, D)),
            _const_spec((1, D)),
            _const_spec((D, D)),
            _const_spec((1, D)),
        ],
        out_specs=pl.BlockSpec((TM, D), row),
        scratch_shapes=[
            pltpu.VMEM((TM, D), jnp.float32),
            pltpu.VMEM((N_LB, CONV_HALO + TM, LANES), jnp.float32),
            pltpu.VMEM((TM, D), jnp.float32),
            pltpu.VMEM((CONV_WIDTH * SUBLANES, D), jnp.float32),
        ],
        compiler_params=pltpu.CompilerParams(
            dimension_semantics=("arbitrary",), vmem_limit_bytes=VMEM_LIMIT),
        name="conv_mixer",
    )(x, meta, g, w1, b1, wdw, bdw, lng, lnb, w2, b2)


def _dense_ffn_kernel(h_ref, g_ref, wg_ref, wu_ref, wd_ref, o_ref):
    h = h_ref[...]
    hn = _rms(h, g_ref[...]).astype(jnp.bfloat16)
    a = jnp.dot(hn, wg_ref[...], preferred_element_type=jnp.float32)
    b = jnp.dot(hn, wu_ref[...], preferred_element_type=jnp.float32)
    mid = (a * jax.nn.sigmoid(a) * b).astype(jnp.bfloat16)
    o_ref[...] = h + jnp.dot(mid, wd_ref[...], preferred_element_type=jnp.float32)


def _dense_ffn(h, g, wg, wu, wd):
    row = lambda i: (i, 0)
    return pl.pallas_call(
        _dense_ffn_kernel,
        out_shape=jax.ShapeDtypeStruct((L, D), jnp.float32),
        grid=(L // TM,),
        in_specs=[
            pl.BlockSpec((TM, D), row),
            _const_spec((1, D)),
            _const_spec((D, D_FF)),
            _const_spec((D, D_FF)),
            _const_spec((D_FF, D)),
        ],
        out_specs=pl.BlockSpec((TM, D), row),
        compiler_params=pltpu.CompilerParams(
            dimension_semantics=("arbitrary",), vmem_limit_bytes=VMEM_LIMIT),
        name="dense_ffn",
    )(h, g, wg, wu, wd)


def _pool_router_kernel(h_ref, gm_ref, wp_ref, sc_ref, gf_ref, wrh_ref, wrl_ref, o_ref, r_ref, pbuf):
    i = pl.program_id(0)

    @pl.when(i == 0)
    def _():
        pbuf[:, 0:POOL_HALO, :] = jnp.zeros((N_LB, POOL_HALO, LANES), jnp.float32)

    h = h_ref[...]
    hn = _rms(h, gm_ref[...])
    for j in range(N_LB):
        pbuf[j, POOL_HALO:POOL_HALO + TM, :] = hn[:, j * LANES:(j + 1) * LANES]
    pos = i * TM + lax.broadcasted_iota(jnp.int32, (TM, 1), 0)
    lb_per_group = POOL_GROUP // LANES
    mixed = []
    for gi, w in enumerate(POOL_WINDOWS):
        inv = 1.0 / jnp.minimum(pos + 1, w).astype(jnp.float32)
        cols = []
        for j in range(gi * lb_per_group, (gi + 1) * lb_per_group):
            x = hn[:, j * LANES:(j + 1) * LANES]
            s = x
            for d in range(1, w):
                s = s + pbuf[j, POOL_HALO - d:POOL_HALO - d + TM, :]
            cols.append(s * inv - x)
        pooled = jnp.concatenate(cols, axis=1).astype(jnp.bfloat16)
        mixed.append(jnp.dot(pooled, wp_ref[gi], preferred_element_type=jnp.float32))
    for j in range(N_LB):
        pbuf[j, 0:POOL_HALO, :] = pbuf[j, TM:TM + POOL_HALO, :]
    h = h + jnp.concatenate(mixed, axis=-1) * sc_ref[...]
    _to_token_tiles(o_ref, 0, h)

    hn2 = _rms(h, gf_ref[...])
    xh = hn2.astype(jnp.bfloat16)
    xl = (hn2 - xh.astype(jnp.float32)).astype(jnp.bfloat16)
    logits = (jnp.dot(xh, wrh_ref[...], preferred_element_type=jnp.float32)
              + jnp.dot(xl, wrh_ref[...], preferred_element_type=jnp.float32)
              + jnp.dot(xh, wrl_ref[...], preferred_element_type=jnp.float32))
    lane = lax.broadcasted_iota(jnp.int32, (TM, ROUTE_LANES), 1)
    neg = jnp.float32(-jnp.inf)
    lg = jnp.where(lane < N_EXPERTS, logits, neg)
    m1 = jnp.max(lg, axis=-1, keepdims=True)
    i1 = jnp.min(jnp.where(lg == m1, lane, ROUTE_LANES), axis=-1, keepdims=True)
    lg2 = jnp.where(lane == i1, neg, lg)
    m2 = jnp.max(lg2, axis=-1, keepdims=True)
    i2 = jnp.min(jnp.where(lg2 == m2, lane, ROUTE_LANES), axis=-1, keepdims=True)
    ex = jnp.exp(m2 - m1)
    w1 = 1.0 / (1.0 + ex)
    w2 = ex / (1.0 + ex)
    r = jnp.where(lane == 0, w1, 0.0)
    r = jnp.where(lane == 1, w2, r)
    r = jnp.where(lane == 2, i1.astype(jnp.float32), r)
    r = jnp.where(lane == 3, i2.astype(jnp.float32), r)
    r_ref[...] = r


def _pool_router(h, gm, wp, sc, gf, wrh, wrl):
    row = lambda i: (i, 0)
    return pl.pallas_call(
        _pool_router_kernel,
        out_shape=(jax.ShapeDtypeStruct((L * SUBLANES, LANES), jnp.float32),
                   jax.ShapeDtypeStruct((L, ROUTE_LANES), jnp.float32)),
        grid=(L // TM,),
        in_specs=[
            pl.BlockSpec((TM, D), row),
            _const_spec((1, D)),
            _const_spec((len(POOL_WINDOWS), POOL_GROUP, POOL_GROUP)),
            _const_spec((1, D)),
            _const_spec((1, D)),
            _const_spec((D, ROUTE_LANES)),
            _const_spec((D, ROUTE_LANES)),
        ],
        out_specs=(pl.BlockSpec((TM * SUBLANES, LANES), row), pl.BlockSpec((TM, ROUTE_LANES), row)),
        scratch_shapes=[pltpu.VMEM((N_LB, POOL_HALO + TM, LANES), jnp.float32)],
        compiler_params=pltpu.CompilerParams(
            dimension_semantics=("arbitrary",), vmem_limit_bytes=VMEM_LIMIT),
        name="pool_router",
    )(h, gm, wp, sc, gf, wrh, wrl)


def _moe_kernel(gexp_ref, gstart_ref, grows_ref, gsrc_ref, sdst_ref,
                h_hbm, g_ref, wg_ref, wu_ref, wd_ref, y_hbm,
                xg, xb, acc, ys, wgb, wub, wdb, gsem, ssem):
    del gexp_ref
    g = pl.program_id(0)
    c = pl.program_id(1)
    rows = grows_ref[g]
    start = gstart_ref[g]
    nsub = (rows + SUB_ROWS - 1) // SUB_ROWS
    first = c == 0
    last = c == N_FF_CHUNKS - 1
    tile = lambda r: pl.ds(pl.multiple_of(r * SUBLANES, SUBLANES), SUBLANES)

    rows8 = lambda v: pl.ds(pl.multiple_of(v, SUBLANES), SUBLANES)

    def scatter_copy(st, j):
        return pltpu.make_async_copy(ys.at[tile(j)], y_hbm.at[rows8(sdst_ref[st + j])], ssem)

    def blocks_wait(sem, nblk):
        def wait_rows(n):
            span = pl.ds(0, n * SUBLANES)
            pltpu.make_async_copy(h_hbm.at[span], xg.at[span], sem).wait()

        def many(b, carry):
            wait_rows(WAIT_BLOCKS * DMA_UNROLL)
            return carry
        lax.fori_loop(0, nblk // WAIT_BLOCKS, many, 0)

        def one(b, carry):
            wait_rows(DMA_UNROLL)
            return carry
        lax.fori_loop(0, nblk % WAIT_BLOCKS, one, 0)

    def gather_start(grp):
        st = gstart_ref[grp]
        nblk = (grows_ref[grp] + DMA_UNROLL - 1) // DMA_UNROLL

        def body(b, carry):
            base = st + b * DMA_UNROLL
            nrow = DMA_UNROLL * SUBLANES
            dst = xg.at[pl.ds(pl.multiple_of(b * nrow, nrow), nrow)]
            for u in range(DMA_UNROLL):
                pltpu.make_async_copy(h_hbm.at[rows8(gsrc_ref[base + u])],
                                      dst.at[pl.ds(u * SUBLANES, SUBLANES)], gsem).start(priority=u % 2)
            return carry
        lax.fori_loop(0, nblk, body, 0)

    def gather_wait(grp):
        blocks_wait(gsem, (grows_ref[grp] + DMA_UNROLL - 1) // DMA_UNROLL)

    def scatter_start(grp):
        st = gstart_ref[grp]
        n = grows_ref[grp]
        nblk = n // DMA_UNROLL

        def body(b, carry):
            for u in range(DMA_UNROLL):
                scatter_copy(st, b * DMA_UNROLL + u).start(priority=u % 2)
            return carry
        lax.fori_loop(0, nblk, body, 0)

        def tail(j, carry):
            scatter_copy(st, j).start()
            return carry
        lax.fori_loop(nblk * DMA_UNROLL, n, tail, 0)

    def scatter_wait(grp):
        n = grows_ref[grp]
        nblk = n // DMA_UNROLL
        blocks_wait(ssem, nblk)

        def tail(j, carry):
            scatter_copy(0, 0).wait()
            return carry
        lax.fori_loop(nblk * DMA_UNROLL, n, tail, 0)

    gprev = jnp.maximum(g - 1, 0)
    gnext = jnp.minimum(g + 1, N_GROUPS - 1)
    prev_live = (g > 0) & (grows_ref[gprev] > 0)

    @pl.when(first & (g == 0) & (rows > 0))
    def _():
        xg[...] = jnp.zeros((GROUP_ROWS * SUBLANES, LANES), jnp.float32)
        gather_start(0)

    @pl.when(first & (rows > 0))
    def _():
        gather_wait(g)

        def norm(s, carry):
            r0 = pl.multiple_of(s * SUB_ROWS, SUB_ROWS)
            x = _from_token_tiles(xg, r0, SUB_ROWS)
            xb[pl.ds(r0, SUB_ROWS), :] = _rms(x, g_ref[...]).astype(jnp.bfloat16)
            return carry
        lax.fori_loop(0, nsub, norm, 0)

        @pl.when((g + 1 < N_GROUPS) & (grows_ref[gnext] > 0))
        def _():
            gather_start(gnext)

    def expert_out(s):
        r0 = pl.multiple_of(s * SUB_ROWS, SUB_ROWS)
        x = xb[pl.ds(r0, SUB_ROWS), :]
        a = jnp.dot(x, wgb[...], preferred_element_type=jnp.float32)
        b = jnp.dot(x, wub[...], preferred_element_type=jnp.float32)
        mid = (a * jax.nn.sigmoid(a) * b).astype(jnp.bfloat16)
        return r0, jnp.dot(mid, wdb[...], preferred_element_type=jnp.float32)

    @pl.when(rows > 0)
    def _():
        wgb[...] = wg_ref[0].astype(jnp.bfloat16)
        wub[...] = wu_ref[0].astype(jnp.bfloat16)
        wdb[...] = wd_ref[0].astype(jnp.bfloat16)

    @pl.when(first & (rows > 0))
    def _():
        def sub(s, carry):
            r0, y = expert_out(s)
            acc[pl.ds(r0, SUB_ROWS), :] = y
            return carry
        lax.fori_loop(0, nsub, sub, 0)

    @pl.when(jnp.logical_not(first) & jnp.logical_not(last) & (rows > 0))
    def _():
        def sub(s, carry):
            r0, y = expert_out(s)
            acc[pl.ds(r0, SUB_ROWS), :] += y
            return carry
        lax.fori_loop(0, nsub, sub, 0)

    @pl.when(last & prev_live)
    def _():
        scatter_wait(gprev)

    @pl.when(last & (rows > 0))
    def _():
        def sub(s, carry):
            r0, y = expert_out(s)
            _to_token_tiles(ys, r0, acc[pl.ds(r0, SUB_ROWS), :] + y)
            return carry
        lax.fori_loop(0, nsub, sub, 0)
        scatter_start(g)

        @pl.when(g == N_GROUPS - 1)
        def _():
            scatter_wait(g)


def _moe(gexp, gstart, grows, gsrc, sdst, h, g, wg, wu, wd):
    def w_in(gi, ci, gexp, gstart, grows, *_):
        return (gexp[gi], 0, jnp.where(grows[gi] > 0, ci, N_FF_CHUNKS - 1))

    def w_out(gi, ci, gexp, gstart, grows, *_):
        return (gexp[gi], jnp.where(grows[gi] > 0, ci, N_FF_CHUNKS - 1), 0)

    grid_spec = pltpu.PrefetchScalarGridSpec(
        num_scalar_prefetch=5,
        grid=(N_GROUPS, N_FF_CHUNKS),
        in_specs=[
            pl.BlockSpec(memory_space=pl.ANY),
            pl.BlockSpec((1, D), lambda gi, ci, *_: (0, 0)),
            pl.BlockSpec((1, D, FF_CHUNK), w_in),
            pl.BlockSpec((1, D, FF_CHUNK), w_in),
            pl.BlockSpec((1, FF_CHUNK, D), w_out),
        ],
        out_specs=pl.BlockSpec(memory_space=pl.ANY),
        scratch_shapes=[
            pltpu.VMEM((GROUP_ROWS * SUBLANES, LANES), jnp.float32),
            pltpu.VMEM((GROUP_ROWS, D), jnp.bfloat16),
            pltpu.VMEM((GROUP_ROWS, D), jnp.float32),
            pltpu.VMEM((GROUP_ROWS * SUBLANES, LANES), jnp.float32),
            pltpu.VMEM((D, FF_CHUNK), jnp.bfloat16),
            pltpu.VMEM((D, FF_CHUNK), jnp.bfloat16),
            pltpu.VMEM((FF_CHUNK, D), jnp.bfloat16),
            pltpu.SemaphoreType.DMA(()),
            pltpu.SemaphoreType.DMA(()),
        ],
    )
    return pl.pallas_call(
        _moe_kernel,
        out_shape=jax.ShapeDtypeStruct((N_ASSIGN * SUBLANES, LANES), jnp.float32),
        grid_spec=grid_spec,
        compiler_params=pltpu.CompilerParams(
            dimension_semantics=("arbitrary", "arbitrary"), vmem_limit_bytes=VMEM_LIMIT),
        name="moe_experts",
    )(gexp, gstart, grows, gsrc, sdst, h, g, wg, wu, wd)


def _final_kernel(h_ref, y0_ref, y1_ref, r_ref, g_ref, o_ref):
    r = r_ref[...]
    h = (_from_token_tiles(h_ref, 0, TF)
         + r[:, 0:1] * _from_token_tiles(y0_ref, 0, TF)
         + r[:, 1:2] * _from_token_tiles(y1_ref, 0, TF))
    o_ref[...] = _rms(h, g_ref[...])


def _final(h, y, r, g):
    return pl.pallas_call(
        _final_kernel,
        out_shape=jax.ShapeDtypeStruct((SEQ, D), jnp.float32),
        grid=(SEQ // TF,),
        in_specs=[
            pl.BlockSpec((pl.Element(TF * SUBLANES), pl.Element(LANES)),
                         lambda i: ((N_META + i * TF) * SUBLANES, 0)),
            pl.BlockSpec((TF * SUBLANES, LANES), lambda i: (i, 0)),
            pl.BlockSpec((TF * SUBLANES, LANES), lambda i: (i + SEQ // TF, 0)),
            pl.BlockSpec((pl.Element(TF), pl.Element(ROUTE_LANES)),
                         lambda i: (SUBLANES * (N_META // SUBLANES + i * (TF // SUBLANES)), 0)),
            _const_spec((1, D)),
        ],
        out_specs=pl.BlockSpec((TF, D), lambda i: (i, 0)),
        compiler_params=pltpu.CompilerParams(
            dimension_semantics=("arbitrary",), vmem_limit_bytes=VMEM_LIMIT),
        name="combine_norm",
    )(h, y, y, r, g)


def _group_table(flat_e):
    order = jnp.argsort(flat_e, stable=True).astype(jnp.int32)
    order = jnp.concatenate([order, jnp.zeros((DMA_UNROLL,), jnp.int32)])
    gsrc = (N_META + (order & (SEQ - 1))) * SUBLANES
    sdst = order * SUBLANES
    experts = jnp.arange(N_EXPERTS, dtype=jnp.int32)
    counts = jnp.sum(flat_e[:, None] == experts[None, :], axis=0, dtype=jnp.int32)
    ngrp = (counts + GROUP_ROWS - 1) // GROUP_ROWS
    cum_grp = jnp.cumsum(ngrp)
    total = cum_grp[-1]
    first_row = jnp.cumsum(counts) - counts
    gi = jnp.arange(N_GROUPS, dtype=jnp.int32)
    valid = gi < total
    gclip = jnp.minimum(gi, total - 1)
    e = jnp.sum(gclip[:, None] >= cum_grp[None, :], axis=1, dtype=jnp.int32)
    e = jnp.clip(e, 0, N_EXPERTS - 1)
    local = gclip - (cum_grp[e] - ngrp[e])
    gstart = jnp.where(valid, first_row[e] + local * GROUP_ROWS, 0).astype(jnp.int32)
    grows = jnp.where(valid, jnp.clip(counts[e] - local * GROUP_ROWS, 0, GROUP_ROWS), 0).astype(jnp.int32)
    return gsrc, sdst, e, gstart, grows


def kernel(x, meta_tokens, conv_w_pw1, conv_b_pw1, conv_w_dw, conv_b_dw, conv_ln_g, conv_ln_b,
           conv_w_pw2, conv_b_pw2, pool_w_group, pool_scale, ffn_w_gate, ffn_w_up, ffn_w_down,
           moe_w_router, moe_w_gate, moe_w_up, moe_w_down, mix_norm_g, ffn_norm_g, final_norm_g):
    bf = jnp.bfloat16
    h = _conv_mixer(x[0], meta_tokens.astype(x.dtype), mix_norm_g[0:1], conv_w_pw1[0].astype(bf),
                    conv_b_pw1, conv_w_dw[0], conv_b_dw, conv_ln_g, conv_ln_b,
                    conv_w_pw2[0].astype(bf), conv_b_pw2)
    h = _dense_ffn(h, ffn_norm_g[0:1], ffn_w_gate[0].astype(bf), ffn_w_up[0].astype(bf),
                   ffn_w_down[0].astype(bf))

    wr = jnp.pad(moe_w_router[0], ((0, 0), (0, ROUTE_LANES - N_EXPERTS)))
    wrh = wr.astype(bf)
    wrl = (wr - wrh.astype(jnp.float32)).astype(bf)
    ht, route = _pool_router(h, mix_norm_g[1:2], pool_w_group[0].astype(bf), pool_scale,
                             ffn_norm_g[1:2], wrh, wrl)

    flat_e = jnp.concatenate([route[N_META:, 2], route[N_META:, 3]]).astype(jnp.int32)
    gsrc, sdst, gexp, gstart, grows = _group_table(flat_e)
    y = _moe(gexp, gstart, grows, gsrc, sdst, ht, ffn_norm_g[1:2], moe_w_gate[0], moe_w_up[0], moe_w_down[0])

    out = _final(ht, y, route, final_norm_g[None, :])
    return out[None]
```

```python
import jax
import jax.numpy as jnp
from jax import lax
from jax.experimental import pallas as pl
from jax.experimental.pallas import tpu as pltpu

D = 1024
SEQ = 16384
N_META = 16
L = SEQ + N_META
CONV_WIDTH = 31
POOL_WINDOWS = (2, 4, 8, 16)
POOL_GROUP = D // len(POOL_WINDOWS)
D_FF = 2816
N_EXPERTS = 8
D_FF_EXPERT = 3584
RMS_EPS = 1e-6
LN_EPS = 1e-5

LANES = 128
SUBLANES = 8
N_LB = D // LANES
TM = 656
CONV_HALO = 32
CONV_ROWS = 16
CONV_PARTS = 4
POOL_HALO = 16
ROUTE_LANES = LANES
GROUP_ROWS = 2560
SUB_ROWS = 640
N_SUB = GROUP_ROWS // SUB_ROWS
FF_CHUNK = 512
N_FF_CHUNKS = D_FF_EXPERT // FF_CHUNK
N_ASSIGN = 2 * SEQ
N_GROUPS = (N_ASSIGN + N_EXPERTS * (GROUP_ROWS - 1)) // GROUP_ROWS
DMA_UNROLL = 16
WAIT_BLOCKS = 8
TF = 1024
VMEM_LIMIT = 56 * 1024 * 1024


def _rms(x, g):
    return x * lax.rsqrt(jnp.mean(x * x, axis=-1, keepdims=True) + RMS_EPS) * g


def _const_spec(shape):
    zeros = (0,) * len(shape)
    return pl.BlockSpec(shape, lambda *_: zeros, pipeline_mode=pl.Buffered(1))


def _to_token_tiles(ref, row0, x):
    n = x.shape[0]
    for j in range(N_LB):
        ref[pl.ds(row0 * SUBLANES + j, n, stride=SUBLANES), :] = x[:, j * LANES:(j + 1) * LANES]


def _from_token_tiles(ref, row0, n):
    return jnp.concatenate(
        [ref[pl.ds(row0 * SUBLANES + j, n, stride=SUBLANES), :] for j in range(N_LB)], axis=1)


def _conv_mixer_kernel(x_ref, meta_ref, g_ref, w1_ref, b1_ref, wdw_ref, bdw_ref, lng_ref, lnb_ref,
                       w2_ref, b2_ref, o_ref, hbuf, ubuf, cbuf, wbc):
    i = pl.program_id(0)

    @pl.when(i == 0)
    def _():
        ubuf[:, 0:CONV_HALO, :] = jnp.zeros((N_LB, CONV_HALO, LANES), jnp.float32)
        for k in range(CONV_WIDTH):
            wbc[k * SUBLANES:(k + 1) * SUBLANES, :] = jnp.broadcast_to(wdw_ref[k:k + 1, :], (SUBLANES, D))
        hbuf[0:N_META, :] = meta_ref[...]
        hbuf[N_META:TM, :] = x_ref[0:TM - N_META, :]

    @pl.when(i > 0)
    def _():
        hbuf[...] = x_ref[...]

    h = hbuf[...]
    hn = _rms(h, g_ref[...]).astype(jnp.bfloat16)
    u = jnp.dot(hn, w1_ref[...], preferred_element_type=jnp.float32) + b1_ref[...]
    u = u[:, :D] * jax.nn.sigmoid(u[:, D:])
    for j in range(N_LB):
        ubuf[j, CONV_HALO:CONV_HALO + TM, :] = u[:, j * LANES:(j + 1) * LANES]

    for j in range(N_LB):
        lanes = slice(j * LANES, (j + 1) * LANES)
        taps = [wbc[k * SUBLANES:(k + 1) * SUBLANES, lanes] for k in range(CONV_WIDTH)]
        bias = jnp.broadcast_to(bdw_ref[:, lanes], (SUBLANES, LANES))

        def chunk(r, carry, j=j, lanes=lanes, taps=taps, bias=bias):
            base = pl.multiple_of(r * CONV_ROWS, CONV_ROWS)
            for q in range(CONV_ROWS // SUBLANES):
                parts = [None] * CONV_PARTS
                for k in range(CONV_WIDTH):
                    d = CONV_WIDTH - 1 - k
                    term = ubuf[j, pl.ds(CONV_HALO + base + q * SUBLANES - d, SUBLANES), :] * taps[k]
                    parts[k % CONV_PARTS] = term if parts[k % CONV_PARTS] is None else parts[k % CONV_PARTS] + term
                cbuf[pl.ds(base + q * SUBLANES, SUBLANES), lanes] = (
                    (parts[0] + parts[1]) + (parts[2] + parts[3]) + bias)
            return carry

        lax.fori_loop(0, TM // CONV_ROWS, chunk, 0)
        ubuf[j, 0:CONV_HALO, :] = ubuf[j, TM:TM + CONV_HALO, :]

    c = cbuf[...]
    mu = jnp.mean(c, axis=-1, keepdims=True)
    cc = c - mu
    var = jnp.mean(cc * cc, axis=-1, keepdims=True)
    y = cc * lax.rsqrt(var + LN_EPS) * lng_ref[...] + lnb_ref[...]
    y = (y * jax.nn.sigmoid(y)).astype(jnp.bfloat16)
    o_ref[...] = h + jnp.dot(y, w2_ref[...], preferred_element_type=jnp.float32) + b2_ref[...]


def _conv_mixer(x, meta, g, w1, b1, wdw, bdw, lng, lnb, w2, b2):
    row = lambda i: (i, 0)
    return pl.pallas_call(
        _conv_mixer_kernel,
        out_shape=jax.ShapeDtypeStruct((L, D), jnp.float32),
        grid=(L // TM,),
        in_specs=[
            pl.BlockSpec((pl.Element(TM), pl.Element(D)),
                         lambda i: (SUBLANES * jnp.maximum(i * (TM // SUBLANES) - N_META // SUBLANES, 0), 0)),
            _const_spec((N_META, D)),
            _const_spec((1, D)),
            _const_spec((D, 2 * D)),
            _const_spec((1, 2 * D)),
            _const_spec((CONV_WIDTH, D)),
            _const_spec((1, D)),
            _const_spec((1, D)),
            _const_spec((1, D)),
            _const_spec((D, D)),
            _const_spec((1, D)),
        ],
        out_specs=pl.BlockSpec((TM, D), row),
        scratch_shapes=[
            pltpu.VMEM((TM, D), jnp.float32),
            pltpu.VMEM((N_LB, CONV_HALO + TM, LANES), jnp.float32),
            pltpu.VMEM((TM, D), jnp.float32),
            pltpu.VMEM((CONV_WIDTH * SUBLANES, D), jnp.float32),
        ],
        compiler_params=pltpu.CompilerParams(
            dimension_semantics=("arbitrary",), vmem_limit_bytes=VMEM_LIMIT),
        name="conv_mixer",
    )(x, meta, g, w1, b1, wdw, bdw, lng, lnb, w2, b2)


def _dense_ffn_kernel(h_ref, g_ref, wg_ref, wu_ref, wd_ref, o_ref):
    h = h_ref[...]
    hn = _rms(h, g_ref[...]).astype(jnp.bfloat16)
    a = jnp.dot(hn, wg_ref[...], preferred_element_type=jnp.float32)
    b = jnp.dot(hn, wu_ref[...], preferred_element_type=jnp.float32)
    mid = (a * jax.nn.sigmoid(a) * b).astype(jnp.bfloat16)
    o_ref[...] = h + jnp.dot(mid, wd_ref[...], preferred_element_type=jnp.float32)


def _dense_ffn(h, g, wg, wu, wd):
    row = lambda i: (i, 0)
    return pl.pallas_call(
        _dense_ffn_kernel,
        out_shape=jax.ShapeDtypeStruct((L, D), jnp.float32),
        grid=(L // TM,),
        in_specs=[
            pl.BlockSpec((TM, D), row),
            _const_spec((1, D)),
            _const_spec((D, D_FF)),
            _const_spec((D, D_FF)),
            _const_spec((D_FF, D)),
        ],
        out_specs=pl.BlockSpec((TM, D), row),
        compiler_params=pltpu.CompilerParams(
            dimension_semantics=("arbitrary",), vmem_limit_bytes=VMEM_LIMIT),
        name="dense_ffn",
    )(h, g, wg, wu, wd)


def _pool_router_kernel(h_ref, gm_ref, wp_ref, sc_ref, gf_ref, wrh_ref, wrl_ref, o_ref, r_ref, pbuf):
    i = pl.program_id(0)

    @pl.when(i == 0)
    def _():
        pbuf[:, 0:POOL_HALO, :] = jnp.zeros((N_LB, POOL_HALO, LANES), jnp.float32)

    h = h_ref[...]
    hn = _rms(h, gm_ref[...])
    for j in range(N_LB):
        pbuf[j, POOL_HALO:POOL_HALO + TM, :] = hn[:, j * LANES:(j + 1) * LANES]
    pos = i * TM + lax.broadcasted_iota(jnp.int32, (TM, 1), 0)
    lb_per_group = POOL_GROUP // LANES
    mixed = []
    for gi, w in enumerate(POOL_WINDOWS):
        inv = 1.0 / jnp.minimum(pos + 1, w).astype(jnp.float32)
        cols = []
        for j in range(gi * lb_per_group, (gi + 1) * lb_per_group):
            x = hn[:, j * LANES:(j + 1) * LANES]
            s = x
            for d in range(1, w):
                s = s + pbuf[j, POOL_HALO - d:POOL_HALO - d + TM, :]
            cols.append(s * inv - x)
        pooled = jnp.concatenate(cols, axis=1).astype(jnp.bfloat16)
        mixed.append(jnp.dot(pooled, wp_ref[gi], preferred_element_type=jnp.float32))
    for j in range(N_LB):
        pbuf[j, 0:POOL_HALO, :] = pbuf[j, TM:TM + POOL_HALO, :]
    h = h + jnp.concatenate(mixed, axis=-1) * sc_ref[...]
    _to_token_tiles(o_ref, 0, h)

    hn2 = _rms(h, gf_ref[...])
    xh = hn2.astype(jnp.bfloat16)
    xl = (hn2 - xh.astype(jnp.float32)).astype(jnp.bfloat16)
    logits = (jnp.dot(xh, wrh_ref[...], preferred_element_type=jnp.float32)
              + jnp.dot(xl, wrh_ref[...], preferred_element_type=jnp.float32)
              + jnp.dot(xh, wrl_ref[...], preferred_element_type=jnp.float32))
    lane = lax.broadcasted_iota(jnp.int32, (TM, ROUTE_LANES), 1)
    neg = jnp.float32(-jnp.inf)
    lg = jnp.where(lane < N_EXPERTS, logits, neg)
    m1 = jnp.max(lg, axis=-1, keepdims=True)
    i1 = jnp.min(jnp.where(lg == m1, lane, ROUTE_LANES), axis=-1, keepdims=True)
    lg2 = jnp.where(lane == i1, neg, lg)
    m2 = jnp.max(lg2, axis=-1, keepdims=True)
    i2 = jnp.min(jnp.where(lg2 == m2, lane, ROUTE_LANES), axis=-1, keepdims=True)
    ex = jnp.exp(m2 - m1)
    w1 = 1.0 / (1.0 + ex)
    w2 = ex / (1.0 + ex)
    r = jnp.where(lane == 0, w1, 0.0)
    r = jnp.where(lane == 1, w2, r)
    r = jnp.where(lane == 2, i1.astype(jnp.float32), r)
    r = jnp.where(lane == 3, i2.astype(jnp.float32), r)
    r_ref[...] = r


def _pool_router(h, gm, wp, sc, gf, wrh, wrl):
    row = lambda i: (i, 0)
    return pl.pallas_call(
        _pool_router_kernel,
        out_shape=(jax.ShapeDtypeStruct((L * SUBLANES, LANES), jnp.float32),
                   jax.ShapeDtypeStruct((L, ROUTE_LANES), jnp.float32)),
        grid=(L // TM,),
        in_specs=[
            pl.BlockSpec((TM, D), row),
            _const_spec((1, D)),
            _const_spec((len(POOL_WINDOWS), POOL_GROUP, POOL_GROUP)),
            _const_spec((1, D)),
            _const_spec((1, D)),
            _const_spec((D, ROUTE_LANES)),
            _const_spec((D, ROUTE_LANES)),
        ],
        out_specs=(pl.BlockSpec((TM * SUBLANES, LANES), row), pl.BlockSpec((TM, ROUTE_LANES), row)),
        scratch_shapes=[pltpu.VMEM((N_LB, POOL_HALO + TM, LANES), jnp.float32)],
        compiler_params=pltpu.CompilerParams(
            dimension_semantics=("arbitrary",), vmem_limit_bytes=VMEM_LIMIT),
        name="pool_router",
    )(h, gm, wp, sc, gf, wrh, wrl)


def _moe_kernel(gexp_ref, gstart_ref, grows_ref, gsrc_ref, sdst_ref,
                h_hbm, g_ref, wg_ref, wu_ref, wd_ref, y_hbm,
                xg, xb, acc, ys, wgb, wub, wdb, gsem, ssem):
    del gexp_ref
    g = pl.program_id(0)
    c = pl.program_id(1)
    rows = grows_ref[g]
    start = gstart_ref[g]
    nsub = (rows + SUB_ROWS - 1) // SUB_ROWS
    first = c == 0
    last = c == N_FF_CHUNKS - 1
    tile = lambda r: pl.ds(pl.multiple_of(r * SUBLANES, SUBLANES), SUBLANES)

    rows8 = lambda v: pl.ds(pl.multiple_of(v, SUBLANES), SUBLANES)

    def scatter_copy(st, j):
        return pltpu.make_async_copy(ys.at[tile(j)], y_hbm.at[rows8(sdst_ref[st + j])], ssem)

    def blocks_wait(sem, nblk):
        def wait_rows(n):
            span = pl.ds(0, n * SUBLANES)
            pltpu.make_async_copy(h_hbm.at[span], xg.at[span], sem).wait()

        def many(b, carry):
            wait_rows(WAIT_BLOCKS * DMA_UNROLL)
            return carry
        lax.fori_loop(0, nblk // WAIT_BLOCKS, many, 0)

        def one(b, carry):
            wait_rows(DMA_UNROLL)
            return carry
        lax.fori_loop(0, nblk % WAIT_BLOCKS, one, 0)

    def gather_start(grp):
        st = gstart_ref[grp]
        nblk = (grows_ref[grp] + DMA_UNROLL - 1) // DMA_UNROLL

        def body(b, carry):
            base = st + b * DMA_UNROLL
            nrow = DMA_UNROLL * SUBLANES
            dst = xg.at[pl.ds(pl.multiple_of(b * nrow, nrow), nrow)]
            for u in range(DMA_UNROLL):
                pltpu.make_async_copy(h_hbm.at[rows8(gsrc_ref[base + u])],
                                      dst.at[pl.ds(u * SUBLANES, SUBLANES)], gsem).start(priority=u % 2)
            return carry
        lax.fori_loop(0, nblk, body, 0)

    def gather_wait(grp):
        blocks_wait(gsem, (grows_ref[grp] + DMA_UNROLL - 1) // DMA_UNROLL)

    def scatter_start(grp):
        st = gstart_ref[grp]
        n = grows_ref[grp]
        nblk = n // DMA_UNROLL

        def body(b, carry):
            for u in range(DMA_UNROLL):
                scatter_copy(st, b * DMA_UNROLL + u).start(priority=u % 2)
            return carry
        lax.fori_loop(0, nblk, body, 0)

        def tail(j, carry):
            scatter_copy(st, j).start()
            return carry
        lax.fori_loop(nblk * DMA_UNROLL, n, tail, 0)

    def scatter_wait(grp):
        n = grows_ref[grp]
        nblk = n // DMA_UNROLL
        blocks_wait(ssem, nblk)

        def tail(j, carry):
            scatter_copy(0, 0).wait()
            return carry
        lax.fori_loop(nblk * DMA_UNROLL, n, tail, 0)

    gprev = jnp.maximum(g - 1, 0)
    gnext = jnp.minimum(g + 1, N_GROUPS - 1)
    prev_live = (g > 0) & (grows_ref[gprev] > 0)

    @pl.when(first & (g == 0) & (rows > 0))
    def _():
        xg[...] = jnp.zeros((GROUP_ROWS * SUBLANES, LANES), jnp.float32)
        gather_start(0)

    @pl.when(first & (rows > 0))
    def _():
        gather_wait(g)

        def norm(s, carry):
            r0 = pl.multiple_of(s * SUB_ROWS, SUB_ROWS)
            x = _from_token_tiles(xg, r0, SUB_ROWS)
            xb[pl.ds(r0, SUB_ROWS), :] = _rms(x, g_ref[...]).astype(jnp.bfloat16)
            return carry
        lax.fori_loop(0, nsub, norm, 0)

        @pl.when((g + 1 < N_GROUPS) & (grows_ref[gnext] > 0))
        def _():
            gather_start(gnext)

    def expert_out(s):
        r0 = pl.multiple_of(s * SUB_ROWS, SUB_ROWS)
        x = xb[pl.ds(r0, SUB_ROWS), :]
        a = jnp.dot(x, wgb[...], preferred_element_type=jnp.float32)
        b = jnp.dot(x, wub[...], preferred_element_type=jnp.float32)
        mid = (a * jax.nn.sigmoid(a) * b).astype(jnp.bfloat16)
        return r0, jnp.dot(mid, wdb[...], preferred_element_type=jnp.float32)

    @pl.when(rows > 0)
    def _():
        wgb[...] = wg_ref[0].astype(jnp.bfloat16)
        wub[...] = wu_ref[0].astype(jnp.bfloat16)
        wdb[...] = wd_ref[0].astype(jnp.bfloat16)

    @pl.when(first & (rows > 0))
    def _():
        def sub(s, carry):
            r0, y = expert_out(s)
            acc[pl.ds(r0, SUB_ROWS), :] = y
            return carry
        lax.fori_loop(0, nsub, sub, 0)

    @pl.when(jnp.logical_not(first) & jnp.logical_not(last) & (rows > 0))
    def _():
        def sub(s, carry):
            r0, y = expert_out(s)
            acc[pl.ds(r0, SUB_ROWS), :] += y
            return carry
        lax.fori_loop(0, nsub, sub, 0)

    @pl.when(last & prev_live)
    def _():
        scatter_wait(gprev)

    @pl.when(last & (rows > 0))
    def _():
        def sub(s, carry):
            r0, y = expert_out(s)
            _to_token_tiles(ys, r0, acc[pl.ds(r0, SUB_ROWS), :] + y)
            return carry
        lax.fori_loop(0, nsub, sub, 0)
        scatter_start(g)

        @pl.when(g == N_GROUPS - 1)
        def _():
            scatter_wait(g)


def _moe(gexp, gstart, grows, gsrc, sdst, h, g, wg, wu, wd):
    def w_in(gi, ci, gexp, gstart, grows, *_):
        return (gexp[gi], 0, jnp.where(grows[gi] > 0, ci, N_FF_CHUNKS - 1))

    def w_out(gi, ci, gexp, gstart, grows, *_):
        return (gexp[gi], jnp.where(grows[gi] > 0, ci, N_FF_CHUNKS - 1), 0)

    grid_spec = pltpu.PrefetchScalarGridSpec(
        num_scalar_prefetch=5,
        grid=(N_GROUPS, N_FF_CHUNKS),
        in_specs=[
            pl.BlockSpec(memory_space=pl.ANY),
            pl.BlockSpec((1, D), lambda gi, ci, *_: (0, 0)),
            pl.BlockSpec((1, D, FF_CHUNK), w_in),
            pl.BlockSpec((1, D, FF_CHUNK), w_in),
            pl.BlockSpec((1, FF_CHUNK, D), w_out),
        ],
        out_specs=pl.BlockSpec(memory_space=pl.ANY),
        scratch_shapes=[
            pltpu.VMEM((GROUP_ROWS * SUBLANES, LANES), jnp.float32),
            pltpu.VMEM((GROUP_ROWS, D), jnp.bfloat16),
            pltpu.VMEM((GROUP_ROWS, D), jnp.float32),
            pltpu.VMEM((GROUP_ROWS * SUBLANES, LANES), jnp.float32),
            pltpu.VMEM((D, FF_CHUNK), jnp.bfloat16),
            pltpu.VMEM((D, FF_CHUNK), jnp.bfloat16),
            pltpu.VMEM((FF_CHUNK, D), jnp.bfloat16),
            pltpu.SemaphoreType.DMA(()),
            pltpu.SemaphoreType.DMA(()),
        ],
    )
    return pl.pallas_call(
        _moe_kernel,
        out_shape=jax.ShapeDtypeStruct((N_ASSIGN * SUBLANES, LANES), jnp.float32),
        grid_spec=grid_spec,
        compiler_params=pltpu.CompilerParams(
            dimension_semantics=("arbitrary", "arbitrary"), vmem_limit_bytes=VMEM_LIMIT),
        name="moe_experts",
    )(gexp, gstart, grows, gsrc, sdst, h, g, wg, wu, wd)


def _final_kernel(h_ref, y0_ref, y1_ref, r_ref, g_ref, o_ref):
    r = r_ref[...]
    h = (_from_token_tiles(h_ref, 0, TF)
         + r[:, 0:1] * _from_token_tiles(y0_ref, 0, TF)
         + r[:, 1:2] * _from_token_tiles(y1_ref, 0, TF))
    o_ref[...] = _rms(h, g_ref[...])


def _final(h, y, r, g):
    return pl.pallas_call(
        _final_kernel,
        out_shape=jax.ShapeDtypeStruct((SEQ, D), jnp.float32),
        grid=(SEQ // TF,),
        in_specs=[
            pl.BlockSpec((pl.Element(TF * SUBLANES), pl.Element(LANES)),
                         lambda i: ((N_META + i * TF) * SUBLANES, 0)),
            pl.BlockSpec((TF * SUBLANES, LANES), lambda i: (i, 0)),
            pl.BlockSpec((TF * SUBLANES, LANES), lambda i: (i + SEQ // TF, 0)),
            pl.BlockSpec((pl.Element(TF), pl.Element(ROUTE_LANES)),
                         lambda i: (SUBLANES * (N_META // SUBLANES + i * (TF // SUBLANES)), 0)),
            _const_spec((1, D)),
        ],
        out_specs=pl.BlockSpec((TF, D), lambda i: (i, 0)),
        compiler_params=pltpu.CompilerParams(
            dimension_semantics=("arbitrary",), vmem_limit_bytes=VMEM_LIMIT),
        name="combine_norm",
    )(h, y, y, r, g)


def _group_table(flat_e):
    order = jnp.argsort(flat_e, stable=True).astype(jnp.int32)
    order = jnp.concatenate([order, jnp.zeros((DMA_UNROLL,), jnp.int32)])
    gsrc = (N_META + (order & (SEQ - 1))) * SUBLANES
    sdst = order * SUBLANES
    experts = jnp.arange(N_EXPERTS, dtype=jnp.int32)
    counts = jnp.sum(flat_e[:, None] == experts[None, :], axis=0, dtype=jnp.int32)
    ngrp = (counts + GROUP_ROWS - 1) // GROUP_ROWS
    cum_grp = jnp.cumsum(ngrp)
    total = cum_grp[-1]
    first_row = jnp.cumsum(counts) - counts
    gi = jnp.arange(N_GROUPS, dtype=jnp.int32)
    valid = gi < total
    gclip = jnp.minimum(gi, total - 1)
    e = jnp.sum(gclip[:, None] >= cum_grp[None, :], axis=1, dtype=jnp.int32)
    e = jnp.clip(e, 0, N_EXPERTS - 1)
    local = gclip - (cum_grp[e] - ngrp[e])
    gstart = jnp.where(valid, first_row[e] + local * GROUP_ROWS, 0).astype(jnp.int32)
    grows = jnp.where(valid, jnp.clip(counts[e] - local * GROUP_ROWS, 0, GROUP_ROWS), 0).astype(jnp.int32)
    return gsrc, sdst, e, gstart, grows


def kernel(x, meta_tokens, conv_w_pw1, conv_b_pw1, conv_w_dw, conv_b_dw, conv_ln_g, conv_ln_b,
           conv_w_pw2, conv_b_pw2, pool_w_group, pool_scale, ffn_w_gate, ffn_w_up, ffn_w_down,
           moe_w_router, moe_w_gate, moe_w_up, moe_w_down, mix_norm_g, ffn_norm_g, final_norm_g):
    bf = jnp.bfloat16
    h = _conv_mixer(x[0], meta_tokens.astype(x.dtype), mix_norm_g[0:1], conv_w_pw1[0].astype(bf),
                    conv_b_pw1, conv_w_dw[0], conv_b_dw, conv_ln_g, conv_ln_b,
                    conv_w_pw2[0].astype(bf), conv_b_pw2)
    h = _dense_ffn(h, ffn_norm_g[0:1], ffn_w_gate[0].astype(bf), ffn_w_up[0].astype(bf),
                   ffn_w_down[0].astype(bf))

    wr = jnp.pad(moe_w_router[0], ((0, 0), (0, ROUTE_LANES - N_EXPERTS)))
    wrh = wr.astype(bf)
    wrl = (wr - wrh.astype(jnp.float32)).astype(bf)
    ht, route = _pool_router(h, mix_norm_g[1:2], pool_w_group[0].astype(bf), pool_scale,
                             ffn_norm_g[1:2], wrh, wrl)

    flat_e = jnp.concatenate([route[N_META:, 2], route[N_META:, 3]]).astype(jnp.int32)
    gsrc, sdst, gexp, gstart, grows = _group_table(flat_e)
    y = _moe(gexp, gstart, grows, gsrc, sdst, ht, ffn_norm_g[1:2], moe_w_gate[0], moe_w_up[0], moe_w_down[0])

    out = _final(ht, y, route, final_norm_g[None, :])
    return out[None]
```

```python
import jax
import jax.numpy as jnp
from jax import lax
from jax.experimental import pallas as pl
from jax.experimental.pallas import tpu as pltpu

D = 1024
SEQ = 16384
N_META = 16
L = SEQ + N_META
CONV_WIDTH = 31
POOL_WINDOWS = (2, 4, 8, 16)
POOL_GROUP = D // len(POOL_WINDOWS)
D_FF = 2816
N_EXPERTS = 8
D_FF_EXPERT = 3584
RMS_EPS = 1e-6
LN_EPS = 1e-5

LANES = 128
SUBLANES = 8
N_LB = D // LANES
TM = 656
CONV_HALO = 32
CONV_ROWS = 16
CONV_PARTS = 4
POOL_HALO = 16
ROUTE_LANES = LANES
GROUP_ROWS = 2560
SUB_ROWS = 640
N_SUB = GROUP_ROWS // SUB_ROWS
FF_CHUNK = 512
N_FF_CHUNKS = D_FF_EXPERT // FF_CHUNK
N_ASSIGN = 2 * SEQ
N_GROUPS = (N_ASSIGN + N_EXPERTS * (GROUP_ROWS - 1)) // GROUP_ROWS
DMA_UNROLL = 32
WAIT_BLOCKS = 4
TF = 1024
VMEM_LIMIT = 56 * 1024 * 1024


def _rms(x, g):
    return x * lax.rsqrt(jnp.mean(x * x, axis=-1, keepdims=True) + RMS_EPS) * g


def _const_spec(shape):
    zeros = (0,) * len(shape)
    return pl.BlockSpec(shape, lambda *_: zeros, pipeline_mode=pl.Buffered(1))


def _to_token_tiles(ref, row0, x):
    n = x.shape[0]
    for j in range(N_LB):
        ref[pl.ds(row0 * SUBLANES + j, n, stride=SUBLANES), :] = x[:, j * LANES:(j + 1) * LANES]


def _from_token_tiles(ref, row0, n):
    return jnp.concatenate(
        [ref[pl.ds(row0 * SUBLANES + j, n, stride=SUBLANES), :] for j in range(N_LB)], axis=1)


def _conv_mixer_kernel(x_ref, meta_ref, g_ref, w1_ref, b1_ref, wdw_ref, bdw_ref, lng_ref, lnb_ref,
                       w2_ref, b2_ref, o_ref, hbuf, ubuf, cbuf, wbc):
    i = pl.program_id(0)

    @pl.when(i == 0)
    def _():
        ubuf[:, 0:CONV_HALO, :] = jnp.zeros((N_LB, CONV_HALO, LANES), jnp.float32)
        for k in range(CONV_WIDTH):
            wbc[k * SUBLANES:(k + 1) * SUBLANES, :] = jnp.broadcast_to(wdw_ref[k:k + 1, :], (SUBLANES, D))
        hbuf[0:N_META, :] = meta_ref[...]
        hbuf[N_META:TM, :] = x_ref[0:TM - N_META, :]

    @pl.when(i > 0)
    def _():
        hbuf[...] = x_ref[...]

    h = hbuf[...]
    hn = _rms(h, g_ref[...]).astype(jnp.bfloat16)
    u = jnp.dot(hn, w1_ref[...], preferred_element_type=jnp.float32) + b1_ref[...]
    u = u[:, :D] * jax.nn.sigmoid(u[:, D:])
    for j in range(N_LB):
        ubuf[j, CONV_HALO:CONV_HALO + TM, :] = u[:, j * LANES:(j + 1) * LANES]

    for j in range(N_LB):
        lanes = slice(j * LANES, (j + 1) * LANES)
        taps = [wbc[k * SUBLANES:(k + 1) * SUBLANES, lanes] for k in range(CONV_WIDTH)]
        bias = jnp.broadcast_to(bdw_ref[:, lanes], (SUBLANES, LANES))

        def chunk(r, carry, j=j, lanes=lanes, taps=taps, bias=bias):
            base = pl.multiple_of(r * CONV_ROWS, CONV_ROWS)
            for q in range(CONV_ROWS // SUBLANES):
                parts = [None] * CONV_PARTS
                for k in range(CONV_WIDTH):
                    d = CONV_WIDTH - 1 - k
                    term = ubuf[j, pl.ds(CONV_HALO + base + q * SUBLANES - d, SUBLANES), :] * taps[k]
                    parts[k % CONV_PARTS] = term if parts[k % CONV_PARTS] is None else parts[k % CONV_PARTS] + term
                cbuf[pl.ds(base + q * SUBLANES, SUBLANES), lanes] = (
                    (parts[0] + parts[1]) + (parts[2] + parts[3]) + bias)
            return carry

        lax.fori_loop(0, TM // CONV_ROWS, chunk, 0)
        ubuf[j, 0:CONV_HALO, :] = ubuf[j, TM:TM + CONV_HALO, :]

    c = cbuf[...]
    mu = jnp.mean(c, axis=-1, keepdims=True)
    cc = c - mu
    var = jnp.mean(cc * cc, axis=-1, keepdims=True)
    y = cc * lax.rsqrt(var + LN_EPS) * lng_ref[...] + lnb_ref[...]
    y = (y * jax.nn.sigmoid(y)).astype(jnp.bfloat16)
    o_ref[...] = h + jnp.dot(y, w2_ref[...], preferred_element_type=jnp.float32) + b2_ref[...]


def _conv_mixer(x, meta, g, w1, b1, wdw, bdw, lng, lnb, w2, b2):
    row = lambda i: (i, 0)
    return pl.pallas_call(
        _conv_mixer_kernel,
        out_shape=jax.ShapeDtypeStruct((L, D), jnp.float32),
        grid=(L // TM,),
        in_specs=[
            pl.BlockSpec((pl.Element(TM), pl.Element(D)),
                         lambda i: (SUBLANES * jnp.maximum(i * (TM // SUBLANES) - N_META // SUBLANES, 0), 0)),
            _const_spec((N_META, D)),
            _const_spec((1, D)),
            _const_spec((D, 2 * D)),
            _const_spec((1, 2 * D)),
            _const_spec((CONV_WIDTH, D)),
            _const_spec((1, D)),
            _const_spec((1, D)),
            _const_spec((1, D)),
            _const_spec((D, D)),
            _const_spec((1, D)),
        ],
        out_specs=pl.BlockSpec((TM, D), row),
        scratch_shapes=[
            pltpu.VMEM((TM, D), jnp.float32),
            pltpu.VMEM((N_LB, CONV_HALO + TM, LANES), jnp.float32),
            pltpu.VMEM((TM, D), jnp.float32),
            pltpu.VMEM((CONV_WIDTH * SUBLANES, D), jnp.float32),
        ],
        compiler_params=pltpu.CompilerParams(
            dimension_semantics=("arbitrary",), vmem_limit_bytes=VMEM_LIMIT),
        name="conv_mixer",
    )(x, meta, g, w1, b1, wdw, bdw, lng, lnb, w2, b2)


def _dense_ffn_kernel(h_ref, g_ref, wg_ref, wu_ref, wd_ref, o_ref):
    h = h_ref[...]
    hn = _rms(h, g_ref[...]).astype(jnp.bfloat16)
    a = jnp.dot(hn, wg_ref[...], preferred_element_type=jnp.float32)
    b = jnp.dot(hn, wu_ref[...], preferred_element_type=jnp.float32)
    mid = (a * jax.nn.sigmoid(a) * b).astype(jnp.bfloat16)
    o_ref[...] = h + jnp.dot(mid, wd_ref[...], preferred_element_type=jnp.float32)


def _dense_ffn(h, g, wg, wu, wd):
    row = lambda i: (i, 0)
    return pl.pallas_call(
        _dense_ffn_kernel,
        out_shape=jax.ShapeDtypeStruct((L, D), jnp.float32),
        grid=(L // TM,),
        in_specs=[
            pl.BlockSpec((TM, D), row),
            _const_spec((1, D)),
            _const_spec((D, D_FF)),
            _const_spec((D, D_FF)),
            _const_spec((D_FF, D)),
        ],
        out_specs=pl.BlockSpec((TM, D), row),
        compiler_params=pltpu.CompilerParams(
            dimension_semantics=("arbitrary",), vmem_limit_bytes=VMEM_LIMIT),
        name="dense_ffn",
    )(h, g, wg, wu, wd)


def _pool_router_kernel(h_ref, gm_ref, wp_ref, sc_ref, gf_ref, wrh_ref, wrl_ref, o_ref, r_ref, pbuf):
    i = pl.program_id(0)

    @pl.when(i == 0)
    def _():
        pbuf[:, 0:POOL_HALO, :] = jnp.zeros((N_LB, POOL_HALO, LANES), jnp.float32)

    h = h_ref[...]
    hn = _rms(h, gm_ref[...])
    for j in range(N_LB):
        pbuf[j, POOL_HALO:POOL_HALO + TM, :] = hn[:, j * LANES:(j + 1) * LANES]
    pos = i * TM + lax.broadcasted_iota(jnp.int32, (TM, 1), 0)
    lb_per_group = POOL_GROUP // LANES
    mixed = []
    for gi, w in enumerate(POOL_WINDOWS):
        inv = 1.0 / jnp.minimum(pos + 1, w).astype(jnp.float32)
        cols = []
        for j in range(gi * lb_per_group, (gi + 1) * lb_per_group):
            x = hn[:, j * LANES:(j + 1) * LANES]
            s = x
            for d in range(1, w):
                s = s + pbuf[j, POOL_HALO - d:POOL_HALO - d + TM, :]
            cols.append(s * inv - x)
        pooled = jnp.concatenate(cols, axis=1).astype(jnp.bfloat16)
        mixed.append(jnp.dot(pooled, wp_ref[gi], preferred_element_type=jnp.float32))
    for j in range(N_LB):
        pbuf[j, 0:POOL_HALO, :] = pbuf[j, TM:TM + POOL_HALO, :]
    h = h + jnp.concatenate(mixed, axis=-1) * sc_ref[...]
    _to_token_tiles(o_ref, 0, h)

    hn2 = _rms(h, gf_ref[...])
    xh = hn2.astype(jnp.bfloat16)
    xl = (hn2 - xh.astype(jnp.float32)).astype(jnp.bfloat16)
    logits = (jnp.dot(xh, wrh_ref[...], preferred_element_type=jnp.float32)
              + jnp.dot(xl, wrh_ref[...], preferred_element_type=jnp.float32)
              + jnp.dot(xh, wrl_ref[...], preferred_element_type=jnp.float32))
    lane = lax.broadcasted_iota(jnp.int32, (TM, ROUTE_LANES), 1)
    neg = jnp.float32(-jnp.inf)
    lg = jnp.where(lane < N_EXPERTS, logits, neg)
    m1 = jnp.max(lg, axis=-1, keepdims=True)
    i1 = jnp.min(jnp.where(lg == m1, lane, ROUTE_LANES), axis=-1, keepdims=True)
    lg2 = jnp.where(lane == i1, neg, lg)
    m2 = jnp.max(lg2, axis=-1, keepdims=True)
    i2 = jnp.min(jnp.where(lg2 == m2, lane, ROUTE_LANES), axis=-1, keepdims=True)
    ex = jnp.exp(m2 - m1)
    w1 = 1.0 / (1.0 + ex)
    w2 = ex / (1.0 + ex)
    r = jnp.where(lane == 0, w1, 0.0)
    r = jnp.where(lane == 1, w2, r)
    r = jnp.where(lane == 2, i1.astype(jnp.float32), r)
    r = jnp.where(lane == 3, i2.astype(jnp.float32), r)
    r_ref[...] = r


def _pool_router(h, gm, wp, sc, gf, wrh, wrl):
    row = lambda i: (i, 0)
    return pl.pallas_call(
        _pool_router_kernel,
        out_shape=(jax.ShapeDtypeStruct((L * SUBLANES, LANES), jnp.float32),
                   jax.ShapeDtypeStruct((L, ROUTE_LANES), jnp.float32)),
        grid=(L // TM,),
        in_specs=[
            pl.BlockSpec((TM, D), row),
            _const_spec((1, D)),
            _const_spec((len(POOL_WINDOWS), POOL_GROUP, POOL_GROUP)),
            _const_spec((1, D)),
            _const_spec((1, D)),
            _const_spec((D, ROUTE_LANES)),
            _const_spec((D, ROUTE_LANES)),
        ],
        out_specs=(pl.BlockSpec((TM * SUBLANES, LANES), row), pl.BlockSpec((TM, ROUTE_LANES), row)),
        scratch_shapes=[pltpu.VMEM((N_LB, POOL_HALO + TM, LANES), jnp.float32)],
        compiler_params=pltpu.CompilerParams(
            dimension_semantics=("arbitrary",), vmem_limit_bytes=VMEM_LIMIT),
        name="pool_router",
    )(h, gm, wp, sc, gf, wrh, wrl)


def _moe_kernel(gexp_ref, gstart_ref, grows_ref, gsrc_ref, sdst_ref,
                h_hbm, g_ref, wg_ref, wu_ref, wd_ref, y_hbm,
                xg, xb, acc, ys, wgb, wub, wdb, gsem, ssem):
    del gexp_ref
    g = pl.program_id(0)
    c = pl.program_id(1)
    rows = grows_ref[g]
    start = gstart_ref[g]
    nsub = (rows + SUB_ROWS - 1) // SUB_ROWS
    first = c == 0
    last = c == N_FF_CHUNKS - 1
    tile = lambda r: pl.ds(pl.multiple_of(r * SUBLANES, SUBLANES), SUBLANES)

    rows8 = lambda v: pl.ds(pl.multiple_of(v, SUBLANES), SUBLANES)

    def scatter_copy(st, j):
        return pltpu.make_async_copy(ys.at[tile(j)], y_hbm.at[rows8(sdst_ref[st + j])], ssem)

    def blocks_wait(sem, nblk):
        def wait_rows(n):
            span = pl.ds(0, n * SUBLANES)
            pltpu.make_async_copy(h_hbm.at[span], xg.at[span], sem).wait()

        def many(b, carry):
            wait_rows(WAIT_BLOCKS * DMA_UNROLL)
            return carry
        lax.fori_loop(0, nblk // WAIT_BLOCKS, many, 0)

        def one(b, carry):
            wait_rows(DMA_UNROLL)
            return carry
        lax.fori_loop(0, nblk % WAIT_BLOCKS, one, 0)

    def gather_start(grp):
        st = gstart_ref[grp]
        nblk = (grows_ref[grp] + DMA_UNROLL - 1) // DMA_UNROLL

        def body(b, carry):
            base = st + b * DMA_UNROLL
            nrow = DMA_UNROLL * SUBLANES
            dst = xg.at[pl.ds(pl.multiple_of(b * nrow, nrow), nrow)]
            for u in range(DMA_UNROLL):
                pltpu.make_async_copy(h_hbm.at[rows8(gsrc_ref[base + u])],
                                      dst.at[pl.ds(u * SUBLANES, SUBLANES)], gsem).start(priority=u % 2)
            return carry
        lax.fori_loop(0, nblk, body, 0)

    def gather_wait(grp):
        blocks_wait(gsem, (grows_ref[grp] + DMA_UNROLL - 1) // DMA_UNROLL)

    def scatter_start(grp):
        st = gstart_ref[grp]
        n = grows_ref[grp]
        nblk = n // DMA_UNROLL

        def body(b, carry):
            for u in range(DMA_UNROLL):
                scatter_copy(st, b * DMA_UNROLL + u).start(priority=u % 2)
            return carry
        lax.fori_loop(0, nblk, body, 0)

        def tail(j, carry):
            scatter_copy(st, j).start()
            return carry
        lax.fori_loop(nblk * DMA_UNROLL, n, tail, 0)

    def scatter_wait(grp):
        n = grows_ref[grp]
        nblk = n // DMA_UNROLL
        blocks_wait(ssem, nblk)

        def tail(j, carry):
            scatter_copy(0, 0).wait()
            return carry
        lax.fori_loop(nblk * DMA_UNROLL, n, tail, 0)

    gprev = jnp.maximum(g - 1, 0)
    gnext = jnp.minimum(g + 1, N_GROUPS - 1)
    prev_live = (g > 0) & (grows_ref[gprev] > 0)

    @pl.when(first & (g == 0) & (rows > 0))
    def _():
        xg[...] = jnp.zeros((GROUP_ROWS * SUBLANES, LANES), jnp.float32)
        gather_start(0)

    @pl.when(first & (rows > 0))
    def _():
        gather_wait(g)

        def norm(s, carry):
            r0 = pl.multiple_of(s * SUB_ROWS, SUB_ROWS)
            x = _from_token_tiles(xg, r0, SUB_ROWS)
            xb[pl.ds(r0, SUB_ROWS), :] = _rms(x, g_ref[...]).astype(jnp.bfloat16)
            return carry
        lax.fori_loop(0, nsub, norm, 0)

        @pl.when((g + 1 < N_GROUPS) & (grows_ref[gnext] > 0))
        def _():
            gather_start(gnext)

    def expert_out(s):
        r0 = pl.multiple_of(s * SUB_ROWS, SUB_ROWS)
        x = xb[pl.ds(r0, SUB_ROWS), :]
        a = jnp.dot(x, wgb[...], preferred_element_type=jnp.float32)
        b = jnp.dot(x, wub[...], preferred_element_type=jnp.float32)
        mid = (a * jax.nn.sigmoid(a) * b).astype(jnp.bfloat16)
        return r0, jnp.dot(mid, wdb[...], preferred_element_type=jnp.float32)

    @pl.when(rows > 0)
    def _():
        wgb[...] = wg_ref[0].astype(jnp.bfloat16)
        wub[...] = wu_ref[0].astype(jnp.bfloat16)
        wdb[...] = wd_ref[0].astype(jnp.bfloat16)

    @pl.when(first & (rows > 0))
    def _():
        def sub(s, carry):
            r0, y = expert_out(s)
            acc[pl.ds(r0, SUB_ROWS), :] = y
            return carry
        lax.fori_loop(0, nsub, sub, 0)

    @pl.when(jnp.logical_not(first) & jnp.logical_not(last) & (rows > 0))
    def _():
        def sub(s, carry):
            r0, y = expert_out(s)
            acc[pl.ds(r0, SUB_ROWS), :] += y
            return carry
        lax.fori_loop(0, nsub, sub, 0)

    @pl.when(last & prev_live)
    def _():
        scatter_wait(gprev)

    @pl.when(last & (rows > 0))
    def _():
        def sub(s, carry):
            r0, y = expert_out(s)
            _to_token_tiles(ys, r0, acc[pl.ds(r0, SUB_ROWS), :] + y)
            return carry
        lax.fori_loop(0, nsub, sub, 0)
        scatter_start(g)

        @pl.when(g == N_GROUPS - 1)
        def _():
            scatter_wait(g)


def _moe(gexp, gstart, grows, gsrc, sdst, h, g, wg, wu, wd):
    def w_in(gi, ci, gexp, gstart, grows, *_):
        return (gexp[gi], 0, jnp.where(grows[gi] > 0, ci, N_FF_CHUNKS - 1))

    def w_out(gi, ci, gexp, gstart, grows, *_):
        return (gexp[gi], jnp.where(grows[gi] > 0, ci, N_FF_CHUNKS - 1), 0)

    grid_spec = pltpu.PrefetchScalarGridSpec(
        num_scalar_prefetch=5,
        grid=(N_GROUPS, N_FF_CHUNKS),
        in_specs=[
            pl.BlockSpec(memory_space=pl.ANY),
            pl.BlockSpec((1, D), lambda gi, ci, *_: (0, 0)),
            pl.BlockSpec((1, D, FF_CHUNK), w_in),
            pl.BlockSpec((1, D, FF_CHUNK), w_in),
            pl.BlockSpec((1, FF_CHUNK, D), w_out),
        ],
        out_specs=pl.BlockSpec(memory_space=pl.ANY),
        scratch_shapes=[
            pltpu.VMEM((GROUP_ROWS * SUBLANES, LANES), jnp.float32),
            pltpu.VMEM((GROUP_ROWS, D), jnp.bfloat16),
            pltpu.VMEM((GROUP_ROWS, D), jnp.float32),
            pltpu.VMEM((GROUP_ROWS * SUBLANES, LANES), jnp.float32),
            pltpu.VMEM((D, FF_CHUNK), jnp.bfloat16),
            pltpu.VMEM((D, FF_CHUNK), jnp.bfloat16),
            pltpu.VMEM((FF_CHUNK, D), jnp.bfloat16),
            pltpu.SemaphoreType.DMA(()),
            pltpu.SemaphoreType.DMA(()),
        ],
    )
    return pl.pallas_call(
        _moe_kernel,
        out_shape=jax.ShapeDtypeStruct((N_ASSIGN * SUBLANES, LANES), jnp.float32),
        grid_spec=grid_spec,
        compiler_params=pltpu.CompilerParams(
            dimension_semantics=("arbitrary", "arbitrary"), vmem_limit_bytes=VMEM_LIMIT),
        name="moe_experts",
    )(gexp, gstart, grows, gsrc, sdst, h, g, wg, wu, wd)


def _final_kernel(h_ref, y0_ref, y1_ref, r_ref, g_ref, o_ref):
    r = r_ref[...]
    h = (_from_token_tiles(h_ref, 0, TF)
         + r[:, 0:1] * _from_token_tiles(y0_ref, 0, TF)
         + r[:, 1:2] * _from_token_tiles(y1_ref, 0, TF))
    o_ref[...] = _rms(h, g_ref[...])


def _final(h, y, r, g):
    return pl.pallas_call(
        _final_kernel,
        out_shape=jax.ShapeDtypeStruct((SEQ, D), jnp.float32),
        grid=(SEQ // TF,),
        in_specs=[
            pl.BlockSpec((pl.Element(TF * SUBLANES), pl.Element(LANES)),
                         lambda i: ((N_META + i * TF) * SUBLANES, 0)),
            pl.BlockSpec((TF * SUBLANES, LANES), lambda i: (i, 0)),
            pl.BlockSpec((TF * SUBLANES, LANES), lambda i: (i + SEQ // TF, 0)),
            pl.BlockSpec((pl.Element(TF), pl.Element(ROUTE_LANES)),
                         lambda i: (SUBLANES * (N_META // SUBLANES + i * (TF // SUBLANES)), 0)),
            _const_spec((1, D)),
        ],
        out_specs=pl.BlockSpec((TF, D), lambda i: (i, 0)),
        compiler_params=pltpu.CompilerParams(
            dimension_semantics=("arbitrary",), vmem_limit_bytes=VMEM_LIMIT),
        name="combine_norm",
    )(h, y, y, r, g)


def _group_table(flat_e):
    order = jnp.argsort(flat_e, stable=True).astype(jnp.int32)
    order = jnp.concatenate([order, jnp.zeros((DMA_UNROLL,), jnp.int32)])
    gsrc = (N_META + (order & (SEQ - 1))) * SUBLANES
    sdst = order * SUBLANES
    experts = jnp.arange(N_EXPERTS, dtype=jnp.int32)
    counts = jnp.sum(flat_e[:, None] == experts[None, :], axis=0, dtype=jnp.int32)
    ngrp = (counts + GROUP_ROWS - 1) // GROUP_ROWS
    cum_grp = jnp.cumsum(ngrp)
    total = cum_grp[-1]
    first_row = jnp.cumsum(counts) - counts
    gi = jnp.arange(N_GROUPS, dtype=jnp.int32)
    valid = gi < total
    gclip = jnp.minimum(gi, total - 1)
    e = jnp.sum(gclip[:, None] >= cum_grp[None, :], axis=1, dtype=jnp.int32)
    e = jnp.clip(e, 0, N_EXPERTS - 1)
    local = gclip - (cum_grp[e] - ngrp[e])
    gstart = jnp.where(valid, first_row[e] + local * GROUP_ROWS, 0).astype(jnp.int32)
    grows = jnp.where(valid, jnp.clip(counts[e] - local * GROUP_ROWS, 0, GROUP_ROWS), 0).astype(jnp.int32)
    return gsrc, sdst, e, gstart, grows


def kernel(x, meta_tokens, conv_w_pw1, conv_b_pw1, conv_w_dw, conv_b_dw, conv_ln_g, conv_ln_b,
           conv_w_pw2, conv_b_pw2, pool_w_group, pool_scale, ffn_w_gate, ffn_w_up, ffn_w_down,
           moe_w_router, moe_w_gate, moe_w_up, moe_w_down, mix_norm_g, ffn_norm_g, final_norm_g):
    bf = jnp.bfloat16
    h = _conv_mixer(x[0], meta_tokens.astype(x.dtype), mix_norm_g[0:1], conv_w_pw1[0].astype(bf),
                    conv_b_pw1, conv_w_dw[0], conv_b_dw, conv_ln_g, conv_ln_b,
                    conv_w_pw2[0].astype(bf), conv_b_pw2)
    h = _dense_ffn(h, ffn_norm_g[0:1], ffn_w_gate[0].astype(bf), ffn_w_up[0].astype(bf),
                   ffn_w_down[0].astype(bf))

    wr = jnp.pad(moe_w_router[0], ((0, 0), (0, ROUTE_LANES - N_EXPERTS)))
    wrh = wr.astype(bf)
    wrl = (wr - wrh.astype(jnp.float32)).astype(bf)
    ht, route = _pool_router(h, mix_norm_g[1:2], pool_w_group[0].astype(bf), pool_scale,
                             ffn_norm_g[1:2], wrh, wrl)

    flat_e = jnp.concatenate([route[N_META:, 2], route[N_META:, 3]]).astype(jnp.int32)
    gsrc, sdst, gexp, gstart, grows = _group_table(flat_e)
    y = _moe(gexp, gstart, grows, gsrc, sdst, ht, ffn_norm_g[1:2], moe_w_gate[0], moe_w_up[0], moe_w_down[0])

    out = _final(ht, y, route, final_norm_g[None, :])
    return out[None]
```

```python
import jax
import jax.numpy as jnp
from jax import lax
from jax.experimental import pallas as pl
from jax.experimental.pallas import tpu as pltpu

D = 1024
SEQ = 16384
N_META = 16
L = SEQ + N_META
CONV_WIDTH = 31
POOL_WINDOWS = (2, 4, 8, 16)
POOL_GROUP = D // len(POOL_WINDOWS)
D_FF = 2816
N_EXPERTS = 8
D_FF_EXPERT = 3584
RMS_EPS = 1e-6
LN_EPS = 1e-5

LANES = 128
SUBLANES = 8
N_LB = D // LANES
TM = 656
CONV_HALO = 32
CONV_ROWS = 16
CONV_PARTS = 4
CONV_UNROLL = 8
POOL_HALO = 16
ROUTE_LANES = LANES
GROUP_ROWS = 2560
SUB_ROWS = 640
N_SUB = GROUP_ROWS // SUB_ROWS
FF_CHUNK = 512
N_FF_CHUNKS = D_FF_EXPERT // FF_CHUNK
N_ASSIGN = 2 * SEQ
N_GROUPS = (N_ASSIGN + N_EXPERTS * (GROUP_ROWS - 1)) // GROUP_ROWS
DMA_UNROLL = 32
WAIT_BLOCKS = 4
TF = 1024
VMEM_LIMIT = 56 * 1024 * 1024


def _rms(x, g):
    return x * lax.rsqrt(jnp.mean(x * x, axis=-1, keepdims=True) + RMS_EPS) * g


def _const_spec(shape):
    zeros = (0,) * len(shape)
    return pl.BlockSpec(shape, lambda *_: zeros, pipeline_mode=pl.Buffered(1))


def _to_token_tiles(ref, row0, x):
    n = x.shape[0]
    for j in range(N_LB):
        ref[pl.ds(row0 * SUBLANES + j, n, stride=SUBLANES), :] = x[:, j * LANES:(j + 1) * LANES]


def _from_token_tiles(ref, row0, n):
    return jnp.concatenate(
        [ref[pl.ds(row0 * SUBLANES + j, n, stride=SUBLANES), :] for j in range(N_LB)], axis=1)


def _conv_mixer_kernel(x_ref, meta_ref, g_ref, w1_ref, b1_ref, wdw_ref, bdw_ref, lng_ref, lnb_ref,
                       w2_ref, b2_ref, o_ref, hbuf, ubuf, cbuf, wbc):
    i = pl.program_id(0)

    @pl.when(i == 0)
    def _():
        ubuf[:, 0:CONV_HALO, :] = jnp.zeros((N_LB, CONV_HALO, LANES), jnp.float32)
        for k in range(CONV_WIDTH):
            wbc[k * SUBLANES:(k + 1) * SUBLANES, :] = jnp.broadcast_to(wdw_ref[k:k + 1, :], (SUBLANES, D))
        hbuf[0:N_META, :] = meta_ref[...]
        hbuf[N_META:TM, :] = x_ref[0:TM - N_META, :]

    @pl.when(i > 0)
    def _():
        hbuf[...] = x_ref[...]

    h = hbuf[...]
    hn = _rms(h, g_ref[...]).astype(jnp.bfloat16)
    u = jnp.dot(hn, w1_ref[...], preferred_element_type=jnp.float32) + b1_ref[...]
    u = u[:, :D] * jax.nn.sigmoid(u[:, D:])
    for j in range(N_LB):
        ubuf[j, CONV_HALO:CONV_HALO + TM, :] = u[:, j * LANES:(j + 1) * LANES]

    for j in range(N_LB):
        lanes = slice(j * LANES, (j + 1) * LANES)
        taps = [wbc[k * SUBLANES:(k + 1) * SUBLANES, lanes] for k in range(CONV_WIDTH)]
        bias = jnp.broadcast_to(bdw_ref[:, lanes], (SUBLANES, LANES))

        def chunk(r, carry, j=j, lanes=lanes, taps=taps, bias=bias):
            base = pl.multiple_of(r * CONV_ROWS, CONV_ROWS)
            for q in range(CONV_ROWS // SUBLANES):
                parts = [None] * CONV_PARTS
                for k in range(CONV_WIDTH):
                    d = CONV_WIDTH - 1 - k
                    term = ubuf[j, pl.ds(CONV_HALO + base + q * SUBLANES - d, SUBLANES), :] * taps[k]
                    parts[k % CONV_PARTS] = term if parts[k % CONV_PARTS] is None else parts[k % CONV_PARTS] + term
                cbuf[pl.ds(base + q * SUBLANES, SUBLANES), lanes] = (
                    (parts[0] + parts[1]) + (parts[2] + parts[3]) + bias)
            return carry

        n_chunks = TM // CONV_ROWS
        n_looped = n_chunks - n_chunks % CONV_UNROLL
        lax.fori_loop(0, n_looped, chunk, 0, unroll=CONV_UNROLL)
        for r in range(n_looped, n_chunks):
            chunk(r, 0)
        ubuf[j, 0:CONV_HALO, :] = ubuf[j, TM:TM + CONV_HALO, :]

    c = cbuf[...]
    mu = jnp.mean(c, axis=-1, keepdims=True)
    cc = c - mu
    var = jnp.mean(cc * cc, axis=-1, keepdims=True)
    y = cc * lax.rsqrt(var + LN_EPS) * lng_ref[...] + lnb_ref[...]
    y = (y * jax.nn.sigmoid(y)).astype(jnp.bfloat16)
    o_ref[...] = h + jnp.dot(y, w2_ref[...], preferred_element_type=jnp.float32) + b2_ref[...]


def _conv_mixer(x, meta, g, w1, b1, wdw, bdw, lng, lnb, w2, b2):
    row = lambda i: (i, 0)
    return pl.pallas_call(
        _conv_mixer_kernel,
        out_shape=jax.ShapeDtypeStruct((L, D), jnp.float32),
        grid=(L // TM,),
        in_specs=[
            pl.BlockSpec((pl.Element(TM), pl.Element(D)),
                         lambda i: (SUBLANES * jnp.maximum(i * (TM // SUBLANES) - N_META // SUBLANES, 0), 0)),
            _const_spec((N_META, D)),
            _const_spec((1, D)),
            _const_spec((D, 2 * D)),
            _const_spec((1, 2 * D)),
            _const_spec((CONV_WIDTH, D)),
            _const_spec((1, D)),
            _const_spec((1, D)),
            _const_spec((1, D)),
            _const_spec((D, D)),
            _const_spec((1, D)),
        ],
        out_specs=pl.BlockSpec((TM, D), row),
        scratch_shapes=[
            pltpu.VMEM((TM, D), jnp.float32),
            pltpu.VMEM((N_LB, CONV_HALO + TM, LANES), jnp.float32),
            pltpu.VMEM((TM, D), jnp.float32),
            pltpu.VMEM((CONV_WIDTH * SUBLANES, D), jnp.float32),
        ],
        compiler_params=pltpu.CompilerParams(
            dimension_semantics=("arbitrary",), vmem_limit_bytes=VMEM_LIMIT),
        name="conv_mixer",
    )(x, meta, g, w1, b1, wdw, bdw, lng, lnb, w2, b2)


def _dense_ffn_kernel(h_ref, g_ref, wg_ref, wu_ref, wd_ref, o_ref):
    h = h_ref[...]
    hn = _rms(h, g_ref[...]).astype(jnp.bfloat16)
    a = jnp.dot(hn, wg_ref[...], preferred_element_type=jnp.float32)
    b = jnp.dot(hn, wu_ref[...], preferred_element_type=jnp.float32)
    mid = (a * jax.nn.sigmoid(a) * b).astype(jnp.bfloat16)
    o_ref[...] = h + jnp.dot(mid, wd_ref[...], preferred_element_type=jnp.float32)


def _dense_ffn(h, g, wg, wu, wd):
    row = lambda i: (i, 0)
    return pl.pallas_call(
        _dense_ffn_kernel,
        out_shape=jax.ShapeDtypeStruct((L, D), jnp.float32),
        grid=(L // TM,),
        in_specs=[
            pl.BlockSpec((TM, D), row),
            _const_spec((1, D)),
            _const_spec((D, D_FF)),
            _const_spec((D, D_FF)),
            _const_spec((D_FF, D)),
        ],
        out_specs=pl.BlockSpec((TM, D), row),
        compiler_params=pltpu.CompilerParams(
            dimension_semantics=("arbitrary",), vmem_limit_bytes=VMEM_LIMIT),
        name="dense_ffn",
    )(h, g, wg, wu, wd)


def _pool_router_kernel(h_ref, gm_ref, wp_ref, sc_ref, gf_ref, wrh_ref, wrl_ref, o_ref, r_ref, pbuf):
    i = pl.program_id(0)

    @pl.when(i == 0)
    def _():
        pbuf[:, 0:POOL_HALO, :] = jnp.zeros((N_LB, POOL_HALO, LANES), jnp.float32)

    h = h_ref[...]
    hn = _rms(h, gm_ref[...])
    for j in range(N_LB):
        pbuf[j, POOL_HALO:POOL_HALO + TM, :] = hn[:, j * LANES:(j + 1) * LANES]
    pos = i * TM + lax.broadcasted_iota(jnp.int32, (TM, 1), 0)
    lb_per_group = POOL_GROUP // LANES
    mixed = []
    for gi, w in enumerate(POOL_WINDOWS):
        inv = 1.0 / jnp.minimum(pos + 1, w).astype(jnp.float32)
        cols = []
        for j in range(gi * lb_per_group, (gi + 1) * lb_per_group):
            x = hn[:, j * LANES:(j + 1) * LANES]
            s = x
            for d in range(1, w):
                s = s + pbuf[j, POOL_HALO - d:POOL_HALO - d + TM, :]
            cols.append(s * inv - x)
        pooled = jnp.concatenate(cols, axis=1).astype(jnp.bfloat16)
        mixed.append(jnp.dot(pooled, wp_ref[gi], preferred_element_type=jnp.float32))
    for j in range(N_LB):
        pbuf[j, 0:POOL_HALO, :] = pbuf[j, TM:TM + POOL_HALO, :]
    h = h + jnp.concatenate(mixed, axis=-1) * sc_ref[...]
    _to_token_tiles(o_ref, 0, h)

    hn2 = _rms(h, gf_ref[...])
    xh = hn2.astype(jnp.bfloat16)
    xl = (hn2 - xh.astype(jnp.float32)).astype(jnp.bfloat16)
    logits = (jnp.dot(xh, wrh_ref[...], preferred_element_type=jnp.float32)
              + jnp.dot(xl, wrh_ref[...], preferred_element_type=jnp.float32)
              + jnp.dot(xh, wrl_ref[...], preferred_element_type=jnp.float32))
    lane = lax.broadcasted_iota(jnp.int32, (TM, ROUTE_LANES), 1)
    neg = jnp.float32(-jnp.inf)
    lg = jnp.where(lane < N_EXPERTS, logits, neg)
    m1 = jnp.max(lg, axis=-1, keepdims=True)
    i1 = jnp.min(jnp.where(lg == m1, lane, ROUTE_LANES), axis=-1, keepdims=True)
    lg2 = jnp.where(lane == i1, neg, lg)
    m2 = jnp.max(lg2, axis=-1, keepdims=True)
    i2 = jnp.min(jnp.where(lg2 == m2, lane, ROUTE_LANES), axis=-1, keepdims=True)
    ex = jnp.exp(m2 - m1)
    w1 = 1.0 / (1.0 + ex)
    w2 = ex / (1.0 + ex)
    r = jnp.where(lane == 0, w1, 0.0)
    r = jnp.where(lane == 1, w2, r)
    r = jnp.where(lane == 2, i1.astype(jnp.float32), r)
    r = jnp.where(lane == 3, i2.astype(jnp.float32), r)
    r_ref[...] = r


def _pool_router(h, gm, wp, sc, gf, wrh, wrl):
    row = lambda i: (i, 0)
    return pl.pallas_call(
        _pool_router_kernel,
        out_shape=(jax.ShapeDtypeStruct((L * SUBLANES, LANES), jnp.float32),
                   jax.ShapeDtypeStruct((L, ROUTE_LANES), jnp.float32)),
        grid=(L // TM,),
        in_specs=[
            pl.BlockSpec((TM, D), row),
            _const_spec((1, D)),
            _const_spec((len(POOL_WINDOWS), POOL_GROUP, POOL_GROUP)),
            _const_spec((1, D)),
            _const_spec((1, D)),
            _const_spec((D, ROUTE_LANES)),
            _const_spec((D, ROUTE_LANES)),
        ],
        out_specs=(pl.BlockSpec((TM * SUBLANES, LANES), row), pl.BlockSpec((TM, ROUTE_LANES), row)),
        scratch_shapes=[pltpu.VMEM((N_LB, POOL_HALO + TM, LANES), jnp.float32)],
        compiler_params=pltpu.CompilerParams(
            dimension_semantics=("arbitrary",), vmem_limit_bytes=VMEM_LIMIT),
        name="pool_router",
    )(h, gm, wp, sc, gf, wrh, wrl)


def _moe_kernel(gexp_ref, gstart_ref, grows_ref, gsrc_ref, sdst_ref,
                h_hbm, g_ref, wg_ref, wu_ref, wd_ref, y_hbm,
                xg, xb, acc, ys, wgb, wub, wdb, gsem, ssem):
    del gexp_ref
    g = pl.program_id(0)
    c = pl.program_id(1)
    rows = grows_ref[g]
    start = gstart_ref[g]
    nsub = (rows + SUB_ROWS - 1) // SUB_ROWS
    first = c == 0
    last = c == N_FF_CHUNKS - 1
    tile = lambda r: pl.ds(pl.multiple_of(r * SUBLANES, SUBLANES), SUBLANES)

    rows8 = lambda v: pl.ds(pl.multiple_of(v, SUBLANES), SUBLANES)

    def scatter_copy(st, j):
        return pltpu.make_async_copy(ys.at[tile(j)], y_hbm.at[rows8(sdst_ref[st + j])], ssem)

    def blocks_wait(sem, nblk):
        def wait_rows(n):
            span = pl.ds(0, n * SUBLANES)
            pltpu.make_async_copy(h_hbm.at[span], xg.at[span], sem).wait()

        def many(b, carry):
            wait_rows(WAIT_BLOCKS * DMA_UNROLL)
            return carry
        lax.fori_loop(0, nblk // WAIT_BLOCKS, many, 0)

        def one(b, carry):
            wait_rows(DMA_UNROLL)
            return carry
        lax.fori_loop(0, nblk % WAIT_BLOCKS, one, 0)

    def gather_start(grp):
        st = gstart_ref[grp]
        nblk = (grows_ref[grp] + DMA_UNROLL - 1) // DMA_UNROLL

        def body(b, carry):
            base = st + b * DMA_UNROLL
            nrow = DMA_UNROLL * SUBLANES
            dst = xg.at[pl.ds(pl.multiple_of(b * nrow, nrow), nrow)]
            for u in range(DMA_UNROLL):
                pltpu.make_async_copy(h_hbm.at[rows8(gsrc_ref[base + u])],
                                      dst.at[pl.ds(u * SUBLANES, SUBLANES)], gsem).start(priority=u % 2)
            return carry
        lax.fori_loop(0, nblk, body, 0)

    def gather_wait(grp):
        blocks_wait(gsem, (grows_ref[grp] + DMA_UNROLL - 1) // DMA_UNROLL)

    def scatter_start(grp):
        st = gstart_ref[grp]
        n = grows_ref[grp]
        nblk = n // DMA_UNROLL

        def body(b, carry):
            for u in range(DMA_UNROLL):
                scatter_copy(st, b * DMA_UNROLL + u).start(priority=u % 2)
            return carry
        lax.fori_loop(0, nblk, body, 0)

        def tail(j, carry):
            scatter_copy(st, j).start()
            return carry
        lax.fori_loop(nblk * DMA_UNROLL, n, tail, 0)

    def scatter_wait(grp):
        n = grows_ref[grp]
        nblk = n // DMA_UNROLL
        blocks_wait(ssem, nblk)

        def tail(j, carry):
            scatter_copy(0, 0).wait()
            return carry
        lax.fori_loop(nblk * DMA_UNROLL, n, tail, 0)

    gprev = jnp.maximum(g - 1, 0)
    gnext = jnp.minimum(g + 1, N_GROUPS - 1)
    prev_live = (g > 0) & (grows_ref[gprev] > 0)

    @pl.when(first & (g == 0) & (rows > 0))
    def _():
        xg[...] = jnp.zeros((GROUP_ROWS * SUBLANES, LANES), jnp.float32)
        gather_start(0)

    @pl.when(first & (rows > 0))
    def _():
        gather_wait(g)

        def norm(s, carry):
            r0 = pl.multiple_of(s * SUB_ROWS, SUB_ROWS)
            x = _from_token_tiles(xg, r0, SUB_ROWS)
            xb[pl.ds(r0, SUB_ROWS), :] = _rms(x, g_ref[...]).astype(jnp.bfloat16)
            return carry
        lax.fori_loop(0, nsub, norm, 0)

        @pl.when((g + 1 < N_GROUPS) & (grows_ref[gnext] > 0))
        def _():
            gather_start(gnext)

    def expert_out(s):
        r0 = pl.multiple_of(s * SUB_ROWS, SUB_ROWS)
        x = xb[pl.ds(r0, SUB_ROWS), :]
        a = jnp.dot(x, wgb[...], preferred_element_type=jnp.float32)
        b = jnp.dot(x, wub[...], preferred_element_type=jnp.float32)
        mid = (a * jax.nn.sigmoid(a) * b).astype(jnp.bfloat16)
        return r0, jnp.dot(mid, wdb[...], preferred_element_type=jnp.float32)

    @pl.when(rows > 0)
    def _():
        wgb[...] = wg_ref[0].astype(jnp.bfloat16)
        wub[...] = wu_ref[0].astype(jnp.bfloat16)
        wdb[...] = wd_ref[0].astype(jnp.bfloat16)

    @pl.when(first & (rows > 0))
    def _():
        def sub(s, carry):
            r0, y = expert_out(s)
            acc[pl.ds(r0, SUB_ROWS), :] = y
            return carry
        lax.fori_loop(0, nsub, sub, 0)

    @pl.when(jnp.logical_not(first) & jnp.logical_not(last) & (rows > 0))
    def _():
        def sub(s, carry):
            r0, y = expert_out(s)
            acc[pl.ds(r0, SUB_ROWS), :] += y
            return carry
        lax.fori_loop(0, nsub, sub, 0)

    @pl.when(last & prev_live)
    def _():
        scatter_wait(gprev)

    @pl.when(last & (rows > 0))
    def _():
        def sub(s, carry):
            r0, y = expert_out(s)
            _to_token_tiles(ys, r0, acc[pl.ds(r0, SUB_ROWS), :] + y)
            return carry
        lax.fori_loop(0, nsub, sub, 0)
        scatter_start(g)

        @pl.when(g == N_GROUPS - 1)
        def _():
            scatter_wait(g)


def _moe(gexp, gstart, grows, gsrc, sdst, h, g, wg, wu, wd):
    def w_in(gi, ci, gexp, gstart, grows, *_):
        return (gexp[gi], 0, jnp.where(grows[gi] > 0, ci, N_FF_CHUNKS - 1))

    def w_out(gi, ci, gexp, gstart, grows, *_):
        return (gexp[gi], jnp.where(grows[gi] > 0, ci, N_FF_CHUNKS - 1), 0)

    grid_spec = pltpu.PrefetchScalarGridSpec(
        num_scalar_prefetch=5,
        grid=(N_GROUPS, N_FF_CHUNKS),
        in_specs=[
            pl.BlockSpec(memory_space=pl.ANY),
            pl.BlockSpec((1, D), lambda gi, ci, *_: (0, 0)),
            pl.BlockSpec((1, D, FF_CHUNK), w_in),
            pl.BlockSpec((1, D, FF_CHUNK), w_in),
            pl.BlockSpec((1, FF_CHUNK, D), w_out),
        ],
        out_specs=pl.BlockSpec(memory_space=pl.ANY),
        scratch_shapes=[
            pltpu.VMEM((GROUP_ROWS * SUBLANES, LANES), jnp.float32),
            pltpu.VMEM((GROUP_ROWS, D), jnp.bfloat16),
            pltpu.VMEM((GROUP_ROWS, D), jnp.float32),
            pltpu.VMEM((GROUP_ROWS * SUBLANES, LANES), jnp.float32),
            pltpu.VMEM((D, FF_CHUNK), jnp.bfloat16),
            pltpu.VMEM((D, FF_CHUNK), jnp.bfloat16),
            pltpu.VMEM((FF_CHUNK, D), jnp.bfloat16),
            pltpu.SemaphoreType.DMA(()),
            pltpu.SemaphoreType.DMA(()),
        ],
    )
    return pl.pallas_call(
        _moe_kernel,
        out_shape=jax.ShapeDtypeStruct((N_ASSIGN * SUBLANES, LANES), jnp.float32),
        grid_spec=grid_spec,
        compiler_params=pltpu.CompilerParams(
            dimension_semantics=("arbitrary", "arbitrary"), vmem_limit_bytes=VMEM_LIMIT),
        name="moe_experts",
    )(gexp, gstart, grows, gsrc, sdst, h, g, wg, wu, wd)


def _final_kernel(h_ref, y0_ref, y1_ref, r_ref, g_ref, o_ref):
    r = r_ref[...]
    h = (_from_token_tiles(h_ref, 0, TF)
         + r[:, 0:1] * _from_token_tiles(y0_ref, 0, TF)
         + r[:, 1:2] * _from_token_tiles(y1_ref, 0, TF))
    o_ref[...] = _rms(h, g_ref[...])


def _final(h, y, r, g):
    return pl.pallas_call(
        _final_kernel,
        out_shape=jax.ShapeDtypeStruct((SEQ, D), jnp.float32),
        grid=(SEQ // TF,),
        in_specs=[
            pl.BlockSpec((pl.Element(TF * SUBLANES), pl.Element(LANES)),
                         lambda i: ((N_META + i * TF) * SUBLANES, 0)),
            pl.BlockSpec((TF * SUBLANES, LANES), lambda i: (i, 0)),
            pl.BlockSpec((TF * SUBLANES, LANES), lambda i: (i + SEQ // TF, 0)),
            pl.BlockSpec((pl.Element(TF), pl.Element(ROUTE_LANES)),
                         lambda i: (SUBLANES * (N_META // SUBLANES + i * (TF // SUBLANES)), 0)),
            _const_spec((1, D)),
        ],
        out_specs=pl.BlockSpec((TF, D), lambda i: (i, 0)),
        compiler_params=pltpu.CompilerParams(
            dimension_semantics=("arbitrary",), vmem_limit_bytes=VMEM_LIMIT),
        name="combine_norm",
    )(h, y, y, r, g)


def _group_table(flat_e):
    order = jnp.argsort(flat_e, stable=True).astype(jnp.int32)
    order = jnp.concatenate([order, jnp.zeros((DMA_UNROLL,), jnp.int32)])
    gsrc = (N_META + (order & (SEQ - 1))) * SUBLANES
    sdst = order * SUBLANES
    experts = jnp.arange(N_EXPERTS, dtype=jnp.int32)
    counts = jnp.sum(flat_e[:, None] == experts[None, :], axis=0, dtype=jnp.int32)
    ngrp = (counts + GROUP_ROWS - 1) // GROUP_ROWS
    cum_grp = jnp.cumsum(ngrp)
    total = cum_grp[-1]
    first_row = jnp.cumsum(counts) - counts
    gi = jnp.arange(N_GROUPS, dtype=jnp.int32)
    valid = gi < total
    gclip = jnp.minimum(gi, total - 1)
    e = jnp.sum(gclip[:, None] >= cum_grp[None, :], axis=1, dtype=jnp.int32)
    e = jnp.clip(e, 0, N_EXPERTS - 1)
    local = gclip - (cum_grp[e] - ngrp[e])
    gstart = jnp.where(valid, first_row[e] + local * GROUP_ROWS, 0).astype(jnp.int32)
    grows = jnp.where(valid, jnp.clip(counts[e] - local * GROUP_ROWS, 0, GROUP_ROWS), 0).astype(jnp.int32)
    return gsrc, sdst, e, gstart, grows


def kernel(x, meta_tokens, conv_w_pw1, conv_b_pw1, conv_w_dw, conv_b_dw, conv_ln_g, conv_ln_b,
           conv_w_pw2, conv_b_pw2, pool_w_group, pool_scale, ffn_w_gate, ffn_w_up, ffn_w_down,
           moe_w_router, moe_w_gate, moe_w_up, moe_w_down, mix_norm_g, ffn_norm_g, final_norm_g):
    bf = jnp.bfloat16
    h = _conv_mixer(x[0], meta_tokens.astype(x.dtype), mix_norm_g[0:1], conv_w_pw1[0].astype(bf),
                    conv_b_pw1, conv_w_dw[0], conv_b_dw, conv_ln_g, conv_ln_b,
                    conv_w_pw2[0].astype(bf), conv_b_pw2)
    h = _dense_ffn(h, ffn_norm_g[0:1], ffn_w_gate[0].astype(bf), ffn_w_up[0].astype(bf),
                   ffn_w_down[0].astype(bf))

    wr = jnp.pad(moe_w_router[0], ((0, 0), (0, ROUTE_LANES - N_EXPERTS)))
    wrh = wr.astype(bf)
    wrl = (wr - wrh.astype(jnp.float32)).astype(bf)
    ht, route = _pool_router(h, mix_norm_g[1:2], pool_w_group[0].astype(bf), pool_scale,
                             ffn_norm_g[1:2], wrh, wrl)

    flat_e = jnp.concatenate([route[N_META:, 2], route[N_META:, 3]]).astype(jnp.int32)
    gsrc, sdst, gexp, gstart, grows = _group_table(flat_e)
    y = _moe(gexp, gstart, grows, gsrc, sdst, ht, ffn_norm_g[1:2], moe_w_gate[0], moe_w_up[0], moe_w_down[0])

    out = _final(ht, y, route, final_norm_g[None, :])
    return out[None]
```

```python
import jax
import jax.numpy as jnp
from jax import lax
from jax.experimental import pallas as pl
from jax.experimental.pallas import tpu as pltpu

D = 1024
SEQ = 16384
N_META = 16
L = SEQ + N_META
CONV_WIDTH = 31
POOL_WINDOWS = (2, 4, 8, 16)
POOL_GROUP = D // len(POOL_WINDOWS)
D_FF = 2816
N_EXPERTS = 8
D_FF_EXPERT = 3584
RMS_EPS = 1e-6
LN_EPS = 1e-5

LANES = 128
SUBLANES = 8
N_LB = D // LANES
TM = 656
CONV_HALO = 32
CONV_ROWS = 16
CONV_PARTS = 4
CONV_UNROLL = 8
POOL_HALO = 16
ROUTE_LANES = LANES
GROUP_ROWS = 2560
SUB_ROWS = 640
N_SUB = GROUP_ROWS // SUB_ROWS
FF_CHUNK = 512
N_FF_CHUNKS = D_FF_EXPERT // FF_CHUNK
N_ASSIGN = 2 * SEQ
N_GROUPS = (N_ASSIGN + N_EXPERTS * (GROUP_ROWS - 1)) // GROUP_ROWS
DMA_UNROLL = 32
WAIT_BLOCKS = 4
TF = 1024
VMEM_LIMIT = 56 * 1024 * 1024


def _rms(x, g):
    return x * lax.rsqrt(jnp.mean(x * x, axis=-1, keepdims=True) + RMS_EPS) * g


def _const_spec(shape):
    zeros = (0,) * len(shape)
    return pl.BlockSpec(shape, lambda *_: zeros, pipeline_mode=pl.Buffered(1))


def _to_token_tiles(ref, row0, x):
    n = x.shape[0]
    for j in range(N_LB):
        ref[pl.ds(row0 * SUBLANES + j, n, stride=SUBLANES), :] = x[:, j * LANES:(j + 1) * LANES]


def _from_token_tiles(ref, row0, n):
    return jnp.concatenate(
        [ref[pl.ds(row0 * SUBLANES + j, n, stride=SUBLANES), :] for j in range(N_LB)], axis=1)


def _conv_mixer_kernel(x_ref, meta_ref, g_ref, w1_ref, b1_ref, wdw_ref, bdw_ref, lng_ref, lnb_ref,
                       w2_ref, b2_ref, o_ref, hbuf, ubuf, cbuf, wbc):
    i = pl.program_id(0)

    @pl.when(i == 0)
    def _():
        ubuf[:, 0:CONV_HALO, :] = jnp.zeros((N_LB, CONV_HALO, LANES), jnp.float32)
        for k in range(CONV_WIDTH):
            wbc[k * SUBLANES:(k + 1) * SUBLANES, :] = jnp.broadcast_to(wdw_ref[k:k + 1, :], (SUBLANES, D))
        hbuf[0:N_META, :] = meta_ref[...]
        hbuf[N_META:TM, :] = x_ref[0:TM - N_META, :]

    @pl.when(i > 0)
    def _():
        hbuf[...] = x_ref[...]

    h = hbuf[...]
    hn = _rms(h, g_ref[...]).astype(jnp.bfloat16)
    u = jnp.dot(hn, w1_ref[...], preferred_element_type=jnp.float32) + b1_ref[...]
    u = u[:, :D] * jax.nn.sigmoid(u[:, D:])
    for j in range(N_LB):
        ubuf[j, CONV_HALO:CONV_HALO + TM, :] = u[:, j * LANES:(j + 1) * LANES]

    for j in range(N_LB):
        lanes = slice(j * LANES, (j + 1) * LANES)
        taps = [wbc[k * SUBLANES:(k + 1) * SUBLANES, lanes] for k in range(CONV_WIDTH)]
        bias = jnp.broadcast_to(bdw_ref[:, lanes], (SUBLANES, LANES))

        def chunk(r, carry, j=j, lanes=lanes, taps=taps, bias=bias):
            base = pl.multiple_of(r * CONV_ROWS, CONV_ROWS)
            for q in range(CONV_ROWS // SUBLANES):
                parts = [None] * CONV_PARTS
                for k in range(CONV_WIDTH):
                    d = CONV_WIDTH - 1 - k
                    term = ubuf[j, pl.ds(CONV_HALO + base + q * SUBLANES - d, SUBLANES), :] * taps[k]
                    parts[k % CONV_PARTS] = term if parts[k % CONV_PARTS] is None else parts[k % CONV_PARTS] + term
                cbuf[pl.ds(base + q * SUBLANES, SUBLANES), lanes] = (
                    (parts[0] + parts[1]) + (parts[2] + parts[3]) + bias)
            return carry

        n_chunks = TM // CONV_ROWS
        n_looped = n_chunks - n_chunks % CONV_UNROLL
        lax.fori_loop(0, n_looped, chunk, 0, unroll=CONV_UNROLL)
        for r in range(n_looped, n_chunks):
            chunk(r, 0)
        ubuf[j, 0:CONV_HALO, :] = ubuf[j, TM:TM + CONV_HALO, :]

    c = cbuf[...]
    mu = jnp.mean(c, axis=-1, keepdims=True)
    cc = c - mu
    var = jnp.mean(cc * cc, axis=-1, keepdims=True)
    y = cc * lax.rsqrt(var + LN_EPS) * lng_ref[...] + lnb_ref[...]
    y = (y * jax.nn.sigmoid(y)).astype(jnp.bfloat16)
    o_ref[...] = h + jnp.dot(y, w2_ref[...], preferred_element_type=jnp.float32) + b2_ref[...]


def _conv_mixer(x, meta, g, w1, b1, wdw, bdw, lng, lnb, w2, b2):
    row = lambda i: (i, 0)
    return pl.pallas_call(
        _conv_mixer_kernel,
        out_shape=jax.ShapeDtypeStruct((L, D), jnp.float32),
        grid=(L // TM,),
        in_specs=[
            pl.BlockSpec((pl.Element(TM), pl.Element(D)),
                         lambda i: (SUBLANES * jnp.maximum(i * (TM // SUBLANES) - N_META // SUBLANES, 0), 0)),
            _const_spec((N_META, D)),
            _const_spec((1, D)),
            _const_spec((D, 2 * D)),
            _const_spec((1, 2 * D)),
            _const_spec((CONV_WIDTH, D)),
            _const_spec((1, D)),
            _const_spec((1, D)),
            _const_spec((1, D)),
            _const_spec((D, D)),
            _const_spec((1, D)),
        ],
        out_specs=pl.BlockSpec((TM, D), row),
        scratch_shapes=[
            pltpu.VMEM((TM, D), jnp.float32),
            pltpu.VMEM((N_LB, CONV_HALO + TM, LANES), jnp.float32),
            pltpu.VMEM((TM, D), jnp.float32),
            pltpu.VMEM((CONV_WIDTH * SUBLANES, D), jnp.float32),
        ],
        compiler_params=pltpu.CompilerParams(
            dimension_semantics=("arbitrary",), vmem_limit_bytes=VMEM_LIMIT),
        name="conv_mixer",
    )(x, meta, g, w1, b1, wdw, bdw, lng, lnb, w2, b2)


def _dense_ffn_kernel(h_ref, g_ref, wg_ref, wu_ref, wd_ref, o_ref):
    h = h_ref[...]
    hn = _rms(h, g_ref[...]).astype(jnp.bfloat16)
    a = jnp.dot(hn, wg_ref[...], preferred_element_type=jnp.float32)
    b = jnp.dot(hn, wu_ref[...], preferred_element_type=jnp.float32)
    mid = (a * jax.nn.sigmoid(a) * b).astype(jnp.bfloat16)
    o_ref[...] = h + jnp.dot(mid, wd_ref[...], preferred_element_type=jnp.float32)


def _dense_ffn(h, g, wg, wu, wd):
    row = lambda i: (i, 0)
    return pl.pallas_call(
        _dense_ffn_kernel,
        out_shape=jax.ShapeDtypeStruct((L, D), jnp.float32),
        grid=(L // TM,),
        in_specs=[
            pl.BlockSpec((TM, D), row),
            _const_spec((1, D)),
            _const_spec((D, D_FF)),
            _const_spec((D, D_FF)),
            _const_spec((D_FF, D)),
        ],
        out_specs=pl.BlockSpec((TM, D), row),
        compiler_params=pltpu.CompilerParams(
            dimension_semantics=("arbitrary",), vmem_limit_bytes=VMEM_LIMIT),
        name="dense_ffn",
    )(h, g, wg, wu, wd)


def _pool_router_kernel(h_ref, gm_ref, wp_ref, sc_ref, gf_ref, wrh_ref, wrl_ref, o_ref, r_ref, pbuf):
    i = pl.program_id(0)

    @pl.when(i == 0)
    def _():
        pbuf[:, 0:POOL_HALO, :] = jnp.zeros((N_LB, POOL_HALO, LANES), jnp.float32)

    h = h_ref[...]
    hn = _rms(h, gm_ref[...])
    for j in range(N_LB):
        pbuf[j, POOL_HALO:POOL_HALO + TM, :] = hn[:, j * LANES:(j + 1) * LANES]
    pos = i * TM + lax.broadcasted_iota(jnp.int32, (TM, 1), 0)
    lb_per_group = POOL_GROUP // LANES
    mixed = []
    for gi, w in enumerate(POOL_WINDOWS):
        inv = 1.0 / jnp.minimum(pos + 1, w).astype(jnp.float32)
        cols = []
        for j in range(gi * lb_per_group, (gi + 1) * lb_per_group):
            x = hn[:, j * LANES:(j + 1) * LANES]
            s = x
            for d in range(1, w):
                s = s + pbuf[j, POOL_HALO - d:POOL_HALO - d + TM, :]
            cols.append(s * inv - x)
        pooled = jnp.concatenate(cols, axis=1).astype(jnp.bfloat16)
        mixed.append(jnp.dot(pooled, wp_ref[gi], preferred_element_type=jnp.float32))
    for j in range(N_LB):
        pbuf[j, 0:POOL_HALO, :] = pbuf[j, TM:TM + POOL_HALO, :]
    h = h + jnp.concatenate(mixed, axis=-1) * sc_ref[...]
    _to_token_tiles(o_ref, 0, h)

    hn2 = _rms(h, gf_ref[...])
    xh = hn2.astype(jnp.bfloat16)
    xl = (hn2 - xh.astype(jnp.float32)).astype(jnp.bfloat16)
    logits = (jnp.dot(xh, wrh_ref[...], preferred_element_type=jnp.float32)
              + jnp.dot(xl, wrh_ref[...], preferred_element_type=jnp.float32)
              + jnp.dot(xh, wrl_ref[...], preferred_element_type=jnp.float32))
    lane = lax.broadcasted_iota(jnp.int32, (TM, ROUTE_LANES), 1)
    neg = jnp.float32(-jnp.inf)
    lg = jnp.where(lane < N_EXPERTS, logits, neg)
    m1 = jnp.max(lg, axis=-1, keepdims=True)
    i1 = jnp.min(jnp.where(lg == m1, lane, ROUTE_LANES), axis=-1, keepdims=True)
    lg2 = jnp.where(lane == i1, neg, lg)
    m2 = jnp.max(lg2, axis=-1, keepdims=True)
    i2 = jnp.min(jnp.where(lg2 == m2, lane, ROUTE_LANES), axis=-1, keepdims=True)
    ex = jnp.exp(m2 - m1)
    w1 = 1.0 / (1.0 + ex)
    w2 = ex / (1.0 + ex)
    r = jnp.where(lane == 0, w1, 0.0)
    r = jnp.where(lane == 1, w2, r)
    r = jnp.where(lane == 2, i1.astype(jnp.float32), r)
    r = jnp.where(lane == 3, i2.astype(jnp.float32), r)
    r_ref[...] = r


def _pool_router(h, gm, wp, sc, gf, wrh, wrl):
    row = lambda i: (i, 0)
    return pl.pallas_call(
        _pool_router_kernel,
        out_shape=(jax.ShapeDtypeStruct((L * SUBLANES, LANES), jnp.float32),
                   jax.ShapeDtypeStruct((L, ROUTE_LANES), jnp.float32)),
        grid=(L // TM,),
        in_specs=[
            pl.BlockSpec((TM, D), row),
            _const_spec((1, D)),
            _const_spec((len(POOL_WINDOWS), POOL_GROUP, POOL_GROUP)),
            _const_spec((1, D)),
            _const_spec((1, D)),
            _const_spec((D, ROUTE_LANES)),
            _const_spec((D, ROUTE_LANES)),
        ],
        out_specs=(pl.BlockSpec((TM * SUBLANES, LANES), row), pl.BlockSpec((TM, ROUTE_LANES), row)),
        scratch_shapes=[pltpu.VMEM((N_LB, POOL_HALO + TM, LANES), jnp.float32)],
        compiler_params=pltpu.CompilerParams(
            dimension_semantics=("arbitrary",), vmem_limit_bytes=VMEM_LIMIT),
        name="pool_router",
    )(h, gm, wp, sc, gf, wrh, wrl)


def _moe_kernel(gexp_ref, gstart_ref, grows_ref, gsrc_ref, sdst_ref,
                h_hbm, g_ref, wg_ref, wu_ref, wd_ref, y_hbm,
                xg, xb, acc, ys, wgb, wub, wdb, gsem, ssem):
    del gexp_ref
    g = pl.program_id(0)
    c = pl.program_id(1)
    rows = grows_ref[g]
    start = gstart_ref[g]
    nsub = (rows + SUB_ROWS - 1) // SUB_ROWS
    first = c == 0
    last = c == N_FF_CHUNKS - 1
    tile = lambda r: pl.ds(pl.multiple_of(r * SUBLANES, SUBLANES), SUBLANES)

    rows8 = lambda v: pl.ds(pl.multiple_of(v, SUBLANES), SUBLANES)

    def scatter_copy(st, j):
        return pltpu.make_async_copy(ys.at[tile(j)], y_hbm.at[rows8(sdst_ref[st + j])], ssem)

    def blocks_wait(sem, nblk):
        def wait_rows(n):
            span = pl.ds(0, n * SUBLANES)
            pltpu.make_async_copy(h_hbm.at[span], xg.at[span], sem).wait()

        def many(b, carry):
            wait_rows(WAIT_BLOCKS * DMA_UNROLL)
            return carry
        lax.fori_loop(0, nblk // WAIT_BLOCKS, many, 0)

        def one(b, carry):
            wait_rows(DMA_UNROLL)
            return carry
        lax.fori_loop(0, nblk % WAIT_BLOCKS, one, 0)

    def gather_start(grp):
        st = gstart_ref[grp]
        nblk = (grows_ref[grp] + DMA_UNROLL - 1) // DMA_UNROLL

        def body(b, carry):
            base = st + b * DMA_UNROLL
            nrow = DMA_UNROLL * SUBLANES
            dst = xg.at[pl.ds(pl.multiple_of(b * nrow, nrow), nrow)]
            for u in range(DMA_UNROLL):
                pltpu.make_async_copy(h_hbm.at[rows8(gsrc_ref[base + u])],
                                      dst.at[pl.ds(u * SUBLANES, SUBLANES)], gsem).start(priority=u % 2)
            return carry
        lax.fori_loop(0, nblk, body, 0)

    def gather_wait(grp):
        blocks_wait(gsem, (grows_ref[grp] + DMA_UNROLL - 1) // DMA_UNROLL)

    def scatter_start(grp):
        st = gstart_ref[grp]
        n = grows_ref[grp]
        nblk = n // DMA_UNROLL

        def body(b, carry):
            for u in range(DMA_UNROLL):
                scatter_copy(st, b * DMA_UNROLL + u).start(priority=u % 2)
            return carry
        lax.fori_loop(0, nblk, body, 0)

        def tail(j, carry):
            scatter_copy(st, j).start()
            return carry
        lax.fori_loop(nblk * DMA_UNROLL, n, tail, 0)

    def scatter_wait(grp):
        n = grows_ref[grp]
        nblk = n // DMA_UNROLL
        blocks_wait(ssem, nblk)

        def tail(j, carry):
            scatter_copy(0, 0).wait()
            return carry
        lax.fori_loop(nblk * DMA_UNROLL, n, tail, 0)

    gprev = jnp.maximum(g - 1, 0)
    gnext = jnp.minimum(g + 1, N_GROUPS - 1)
    prev_live = (g > 0) & (grows_ref[gprev] > 0)

    @pl.when(first & (g == 0) & (rows > 0))
    def _():
        xg[...] = jnp.zeros((GROUP_ROWS * SUBLANES, LANES), jnp.float32)
        gather_start(0)

    @pl.when(first & (rows > 0))
    def _():
        gather_wait(g)

        def norm(s):
            r0 = s * SUB_ROWS if isinstance(s, int) else pl.multiple_of(s * SUB_ROWS, SUB_ROWS)
            x = _from_token_tiles(xg, r0, SUB_ROWS)
            xb[pl.ds(r0, SUB_ROWS), :] = _rms(x, g_ref[...]).astype(jnp.bfloat16)

        @pl.when(nsub == N_SUB)
        def _():
            for s in range(N_SUB):
                norm(s)

        @pl.when(nsub < N_SUB)
        def _():
            def norm_pair(k, carry):
                norm(2 * k)
                norm(2 * k + 1)
                return carry
            lax.fori_loop(0, nsub // 2, norm_pair, 0)

            @pl.when(nsub % 2 == 1)
            def _():
                norm(nsub - 1)

        @pl.when((g + 1 < N_GROUPS) & (grows_ref[gnext] > 0))
        def _():
            gather_start(gnext)

    def expert_out(s):
        r0 = s * SUB_ROWS if isinstance(s, int) else pl.multiple_of(s * SUB_ROWS, SUB_ROWS)
        x = xb[pl.ds(r0, SUB_ROWS), :]
        a = jnp.dot(x, wgb[...], preferred_element_type=jnp.float32)
        b = jnp.dot(x, wub[...], preferred_element_type=jnp.float32)
        mid = (a * jax.nn.sigmoid(a) * b).astype(jnp.bfloat16)
        return r0, jnp.dot(mid, wdb[...], preferred_element_type=jnp.float32)

    @pl.when(rows > 0)
    def _():
        wgb[...] = wg_ref[0].astype(jnp.bfloat16)
        wub[...] = wu_ref[0].astype(jnp.bfloat16)
        wdb[...] = wd_ref[0].astype(jnp.bfloat16)

    def chunk_pass(update):
        @pl.when(nsub == N_SUB)
        def _():
            for s in range(N_SUB):
                update(*expert_out(s))

        @pl.when(nsub < N_SUB)
        def _():
            def pair(k, carry):
                update(*expert_out(2 * k))
                update(*expert_out(2 * k + 1))
                return carry
            lax.fori_loop(0, nsub // 2, pair, 0)

            @pl.when(nsub % 2 == 1)
            def _():
                update(*expert_out(nsub - 1))

    @pl.when(first & (rows > 0))
    def _():
        def update(r0, y):
            acc[pl.ds(r0, SUB_ROWS), :] = y
        chunk_pass(update)

    @pl.when(jnp.logical_not(first) & jnp.logical_not(last) & (rows > 0))
    def _():
        def update(r0, y):
            acc[pl.ds(r0, SUB_ROWS), :] += y
        chunk_pass(update)

    @pl.when(last & prev_live)
    def _():
        scatter_wait(gprev)

    @pl.when(last & (rows > 0))
    def _():
        def update(r0, y):
            _to_token_tiles(ys, r0, acc[pl.ds(r0, SUB_ROWS), :] + y)
        chunk_pass(update)
        scatter_start(g)

        @pl.when(g == N_GROUPS - 1)
        def _():
            scatter_wait(g)


def _moe(gexp, gstart, grows, gsrc, sdst, h, g, wg, wu, wd):
    def w_in(gi, ci, gexp, gstart, grows, *_):
        return (gexp[gi], 0, jnp.where(grows[gi] > 0, ci, N_FF_CHUNKS - 1))

    def w_out(gi, ci, gexp, gstart, grows, *_):
        return (gexp[gi], jnp.where(grows[gi] > 0, ci, N_FF_CHUNKS - 1), 0)

    grid_spec = pltpu.PrefetchScalarGridSpec(
        num_scalar_prefetch=5,
        grid=(N_GROUPS, N_FF_CHUNKS),
        in_specs=[
            pl.BlockSpec(memory_space=pl.ANY),
            pl.BlockSpec((1, D), lambda gi, ci, *_: (0, 0)),
            pl.BlockSpec((1, D, FF_CHUNK), w_in),
            pl.BlockSpec((1, D, FF_CHUNK), w_in),
            pl.BlockSpec((1, FF_CHUNK, D), w_out),
        ],
        out_specs=pl.BlockSpec(memory_space=pl.ANY),
        scratch_shapes=[
            pltpu.VMEM((GROUP_ROWS * SUBLANES, LANES), jnp.float32),
            pltpu.VMEM((GROUP_ROWS, D), jnp.bfloat16),
            pltpu.VMEM((GROUP_ROWS, D), jnp.float32),
            pltpu.VMEM((GROUP_ROWS * SUBLANES, LANES), jnp.float32),
            pltpu.VMEM((D, FF_CHUNK), jnp.bfloat16),
            pltpu.VMEM((D, FF_CHUNK), jnp.bfloat16),
            pltpu.VMEM((FF_CHUNK, D), jnp.bfloat16),
            pltpu.SemaphoreType.DMA(()),
            pltpu.SemaphoreType.DMA(()),
        ],
    )
    return pl.pallas_call(
        _moe_kernel,
        out_shape=jax.ShapeDtypeStruct((N_ASSIGN * SUBLANES, LANES), jnp.float32),
        grid_spec=grid_spec,
        compiler_params=pltpu.CompilerParams(
            dimension_semantics=("arbitrary", "arbitrary"), vmem_limit_bytes=VMEM_LIMIT),
        name="moe_experts",
    )(gexp, gstart, grows, gsrc, sdst, h, g, wg, wu, wd)


def _final_kernel(h_ref, y0_ref, y1_ref, r_ref, g_ref, o_ref):
    r = r_ref[...]
    h = (_from_token_tiles(h_ref, 0, TF)
         + r[:, 0:1] * _from_token_tiles(y0_ref, 0, TF)
         + r[:, 1:2] * _from_token_tiles(y1_ref, 0, TF))
    o_ref[...] = _rms(h, g_ref[...])


def _final(h, y, r, g):
    return pl.pallas_call(
        _final_kernel,
        out_shape=jax.ShapeDtypeStruct((SEQ, D), jnp.float32),
        grid=(SEQ // TF,),
        in_specs=[
            pl.BlockSpec((pl.Element(TF * SUBLANES), pl.Element(LANES)),
                         lambda i: ((N_META + i * TF) * SUBLANES, 0)),
            pl.BlockSpec((TF * SUBLANES, LANES), lambda i: (i, 0)),
            pl.BlockSpec((TF * SUBLANES, LANES), lambda i: (i + SEQ // TF, 0)),
            pl.BlockSpec((pl.Element(TF), pl.Element(ROUTE_LANES)),
                         lambda i: (SUBLANES * (N_META // SUBLANES + i * (TF // SUBLANES)), 0)),
            _const_spec((1, D)),
        ],
        out_specs=pl.BlockSpec((TF, D), lambda i: (i, 0)),
        compiler_params=pltpu.CompilerParams(
            dimension_semantics=("arbitrary",), vmem_limit_bytes=VMEM_LIMIT),
        name="combine_norm",
    )(h, y, y, r, g)


def _group_table(flat_e):
    order = jnp.argsort(flat_e, stable=True).astype(jnp.int32)
    order = jnp.concatenate([order, jnp.zeros((DMA_UNROLL,), jnp.int32)])
    gsrc = (N_META + (order & (SEQ - 1))) * SUBLANES
    sdst = order * SUBLANES
    experts = jnp.arange(N_EXPERTS, dtype=jnp.int32)
    counts = jnp.sum(flat_e[:, None] == experts[None, :], axis=0, dtype=jnp.int32)
    ngrp = (counts + GROUP_ROWS - 1) // GROUP_ROWS
    cum_grp = jnp.cumsum(ngrp)
    total = cum_grp[-1]
    first_row = jnp.cumsum(counts) - counts
    gi = jnp.arange(N_GROUPS, dtype=jnp.int32)
    valid = gi < total
    gclip = jnp.minimum(gi, total - 1)
    e = jnp.sum(gclip[:, None] >= cum_grp[None, :], axis=1, dtype=jnp.int32)
    e = jnp.clip(e, 0, N_EXPERTS - 1)
    local = gclip - (cum_grp[e] - ngrp[e])
    gstart = jnp.where(valid, first_row[e] + local * GROUP_ROWS, 0).astype(jnp.int32)
    grows = jnp.where(valid, jnp.clip(counts[e] - local * GROUP_ROWS, 0, GROUP_ROWS), 0).astype(jnp.int32)
    return gsrc, sdst, e, gstart, grows


def kernel(x, meta_tokens, conv_w_pw1, conv_b_pw1, conv_w_dw, conv_b_dw, conv_ln_g, conv_ln_b,
           conv_w_pw2, conv_b_pw2, pool_w_group, pool_scale, ffn_w_gate, ffn_w_up, ffn_w_down,
           moe_w_router, moe_w_gate, moe_w_up, moe_w_down, mix_norm_g, ffn_norm_g, final_norm_g):
    bf = jnp.bfloat16
    h = _conv_mixer(x[0], meta_tokens.astype(x.dtype), mix_norm_g[0:1], conv_w_pw1[0].astype(bf),
                    conv_b_pw1, conv_w_dw[0], conv_b_dw, conv_ln_g, conv_ln_b,
                    conv_w_pw2[0].astype(bf), conv_b_pw2)
    h = _dense_ffn(h, ffn_norm_g[0:1], ffn_w_gate[0].astype(bf), ffn_w_up[0].astype(bf),
                   ffn_w_down[0].astype(bf))

    wr = jnp.pad(moe_w_router[0], ((0, 0), (0, ROUTE_LANES - N_EXPERTS)))
    wrh = wr.astype(bf)
    wrl = (wr - wrh.astype(jnp.float32)).astype(bf)
    ht, route = _pool_router(h, mix_norm_g[1:2], pool_w_group[0].astype(bf), pool_scale,
                             ffn_norm_g[1:2], wrh, wrl)

    flat_e = jnp.concatenate([route[N_META:, 2], route[N_META:, 3]]).astype(jnp.int32)
    gsrc, sdst, gexp, gstart, grows = _group_table(flat_e)
    y = _moe(gexp, gstart, grows, gsrc, sdst, ht, ffn_norm_g[1:2], moe_w_gate[0], moe_w_up[0], moe_w_down[0])

    out = _final(ht, y, route, final_norm_g[None, :])
    return out[None]
```

```python
import jax
import jax.numpy as jnp
from jax import lax
from jax.experimental import pallas as pl
from jax.experimental.pallas import tpu as pltpu

D = 1024
SEQ = 16384
N_META = 16
L = SEQ + N_META
CONV_WIDTH = 31
POOL_WINDOWS = (2, 4, 8, 16)
POOL_GROUP = D // len(POOL_WINDOWS)
D_FF = 2816
N_EXPERTS = 8
D_FF_EXPERT = 3584
RMS_EPS = 1e-6
LN_EPS = 1e-5

LANES = 128
SUBLANES = 8
N_LB = D // LANES
TM = 656
CONV_HALO = 32
CONV_ROWS = 16
CONV_PARTS = 4
CONV_UNROLL = 8
ROW_SPLIT = ((0, 320), (320, TM))
POOL_HALO = 16
ROUTE_LANES = LANES
GROUP_ROWS = 2560
SUB_ROWS = 640
FF_CHUNK = 512
N_FF_CHUNKS = D_FF_EXPERT // FF_CHUNK
N_ASSIGN = 2 * SEQ
N_GROUPS = (N_ASSIGN + N_EXPERTS * (GROUP_ROWS - 1)) // GROUP_ROWS
DMA_UNROLL = 32
WAIT_BLOCKS = 4
TF = 1024
VMEM_LIMIT = 56 * 1024 * 1024

assert SEQ & (SEQ - 1) == 0 and L % TM == 0 and SEQ % TF == 0 and GROUP_ROWS % SUB_ROWS == 0
assert GROUP_ROWS % DMA_UNROLL == 0 and D_FF_EXPERT % FF_CHUNK == 0 and N_LB == SUBLANES


def _rms(x, g):
    return x * lax.rsqrt(jnp.mean(x * x, axis=-1, keepdims=True) + RMS_EPS) * g


def _const_spec(shape):
    zeros = (0,) * len(shape)
    return pl.BlockSpec(shape, lambda *_: zeros, pipeline_mode=pl.Buffered(1))


def _to_token_tiles(ref, row0, x):
    n = x.shape[0]
    for j in range(N_LB):
        ref[pl.ds(row0 * SUBLANES + j, n, stride=SUBLANES), :] = x[:, j * LANES:(j + 1) * LANES]


def _from_token_tiles(ref, row0, n):
    return jnp.concatenate(
        [ref[pl.ds(row0 * SUBLANES + j, n, stride=SUBLANES), :] for j in range(N_LB)], axis=1)


def _conv_mixer_kernel(x_ref, meta_ref, g_ref, w1_ref, b1_ref, wdw_ref, bdw_ref, lng_ref, lnb_ref,
                       w2_ref, b2_ref, o_ref, hbuf, ubuf, cbuf, wbc):
    i = pl.program_id(0)

    @pl.when(i == 0)
    def _():
        ubuf[:, 0:CONV_HALO, :] = jnp.zeros((N_LB, CONV_HALO, LANES), jnp.float32)
        for k in range(CONV_WIDTH):
            wbc[k * SUBLANES:(k + 1) * SUBLANES, :] = jnp.broadcast_to(wdw_ref[k:k + 1, :], (SUBLANES, D))
        hbuf[0:N_META, :] = meta_ref[...]
        hbuf[N_META:TM, :] = x_ref[0:TM - N_META, :]

    @pl.when(i > 0)
    def _():
        hbuf[...] = x_ref[...]

    for lo, hi in ROW_SPLIT:
        hn = _rms(hbuf[lo:hi, :], g_ref[...]).astype(jnp.bfloat16)
        u = jnp.dot(hn, w1_ref[...], preferred_element_type=jnp.float32) + b1_ref[...]
        u = u[:, :D] * jax.nn.sigmoid(u[:, D:])
        for j in range(N_LB):
            ubuf[j, CONV_HALO + lo:CONV_HALO + hi, :] = u[:, j * LANES:(j + 1) * LANES]

    for j in range(N_LB):
        lanes = slice(j * LANES, (j + 1) * LANES)
        taps = [wbc[k * SUBLANES:(k + 1) * SUBLANES, lanes] for k in range(CONV_WIDTH)]
        bias = jnp.broadcast_to(bdw_ref[:, lanes], (SUBLANES, LANES))

        def chunk(r, carry, j=j, lanes=lanes, taps=taps, bias=bias):
            base = pl.multiple_of(r * CONV_ROWS, CONV_ROWS)
            for q in range(CONV_ROWS // SUBLANES):
                parts = [None] * CONV_PARTS
                for k in range(CONV_WIDTH):
                    d = CONV_WIDTH - 1 - k
                    term = ubuf[j, pl.ds(CONV_HALO + base + q * SUBLANES - d, SUBLANES), :] * taps[k]
                    parts[k % CONV_PARTS] = term if parts[k % CONV_PARTS] is None else parts[k % CONV_PARTS] + term
                cbuf[pl.ds(base + q * SUBLANES, SUBLANES), lanes] = (
                    (parts[0] + parts[1]) + (parts[2] + parts[3]) + bias)
            return carry

        n_chunks = TM // CONV_ROWS
        n_looped = n_chunks - n_chunks % CONV_UNROLL
        lax.fori_loop(0, n_looped, chunk, 0, unroll=CONV_UNROLL)
        for r in range(n_looped, n_chunks):
            chunk(r, 0)
        ubuf[j, 0:CONV_HALO, :] = ubuf[j, TM:TM + CONV_HALO, :]

    for lo, hi in ROW_SPLIT:
        c = cbuf[lo:hi, :]
        mu = jnp.mean(c, axis=-1, keepdims=True)
        cc = c - mu
        var = jnp.mean(cc * cc, axis=-1, keepdims=True)
        y = cc * lax.rsqrt(var + LN_EPS) * lng_ref[...] + lnb_ref[...]
        y = (y * jax.nn.sigmoid(y)).astype(jnp.bfloat16)
        o_ref[lo:hi, :] = (hbuf[lo:hi, :] + jnp.dot(y, w2_ref[...], preferred_element_type=jnp.float32)
                           + b2_ref[...])


def _conv_mixer(x, meta, g, w1, b1, wdw, bdw, lng, lnb, w2, b2):
    row = lambda i: (i, 0)
    return pl.pallas_call(
        _conv_mixer_kernel,
        out_shape=jax.ShapeDtypeStruct((L, D), jnp.float32),
        grid=(L // TM,),
        in_specs=[
            pl.BlockSpec((pl.Element(TM), pl.Element(D)),
                         lambda i: (SUBLANES * jnp.maximum(i * (TM // SUBLANES) - N_META // SUBLANES, 0), 0)),
            _const_spec((N_META, D)),
            _const_spec((1, D)),
            _const_spec((D, 2 * D)),
            _const_spec((1, 2 * D)),
            _const_spec((CONV_WIDTH, D)),
            _const_spec((1, D)),
            _const_spec((1, D)),
            _const_spec((1, D)),
            _const_spec((D, D)),
            _const_spec((1, D)),
        ],
        out_specs=pl.BlockSpec((TM, D), row),
        scratch_shapes=[
            pltpu.VMEM((TM, D), jnp.float32),
            pltpu.VMEM((N_LB, CONV_HALO + TM, LANES), jnp.float32),
            pltpu.VMEM((TM, D), jnp.float32),
            pltpu.VMEM((CONV_WIDTH * SUBLANES, D), jnp.float32),
        ],
        compiler_params=pltpu.CompilerParams(
            dimension_semantics=("arbitrary",), vmem_limit_bytes=VMEM_LIMIT),
        name="conv_mixer",
    )(x, meta, g, w1, b1, wdw, bdw, lng, lnb, w2, b2)


def _dense_ffn_kernel(h_ref, g_ref, wg_ref, wu_ref, wd_ref, o_ref):
    for lo, hi in ROW_SPLIT:
        h = h_ref[lo:hi, :]
        hn = _rms(h, g_ref[...]).astype(jnp.bfloat16)
        a = jnp.dot(hn, wg_ref[...], preferred_element_type=jnp.float32)
        b = jnp.dot(hn, wu_ref[...], preferred_element_type=jnp.float32)
        mid = (a * jax.nn.sigmoid(a) * b).astype(jnp.bfloat16)
        o_ref[lo:hi, :] = h + jnp.dot(mid, wd_ref[...], preferred_element_type=jnp.float32)


def _dense_ffn(h, g, wg, wu, wd):
    row = lambda i: (i, 0)
    return pl.pallas_call(
        _dense_ffn_kernel,
        out_shape=jax.ShapeDtypeStruct((L, D), jnp.float32),
        grid=(L // TM,),
        in_specs=[
            pl.BlockSpec((TM, D), row),
            _const_spec((1, D)),
            _const_spec((D, D_FF)),
            _const_spec((D, D_FF)),
            _const_spec((D_FF, D)),
        ],
        out_specs=pl.BlockSpec((TM, D), row),
        compiler_params=pltpu.CompilerParams(
            dimension_semantics=("arbitrary",), vmem_limit_bytes=VMEM_LIMIT),
        name="dense_ffn",
    )(h, g, wg, wu, wd)


def _pool_router_kernel(h_ref, gm_ref, wp_ref, sc_ref, gf_ref, wrh_ref, wrl_ref, o_ref, r_ref, pbuf):
    i = pl.program_id(0)

    @pl.when(i == 0)
    def _():
        pbuf[:, 0:POOL_HALO, :] = jnp.zeros((N_LB, POOL_HALO, LANES), jnp.float32)

    h = h_ref[...]
    hn = _rms(h, gm_ref[...])
    for j in range(N_LB):
        pbuf[j, POOL_HALO:POOL_HALO + TM, :] = hn[:, j * LANES:(j + 1) * LANES]
    pos = i * TM + lax.broadcasted_iota(jnp.int32, (TM, 1), 0)
    lb_per_group = POOL_GROUP // LANES
    mixed = []
    for gi, w in enumerate(POOL_WINDOWS):
        inv = 1.0 / jnp.minimum(pos + 1, w).astype(jnp.float32)
        cols = []
        for j in range(gi * lb_per_group, (gi + 1) * lb_per_group):
            x = hn[:, j * LANES:(j + 1) * LANES]
            s = x
            for d in range(1, w):
                s = s + pbuf[j, POOL_HALO - d:POOL_HALO - d + TM, :]
            cols.append(s * inv - x)
        pooled = jnp.concatenate(cols, axis=1).astype(jnp.bfloat16)
        mixed.append(jnp.dot(pooled, wp_ref[gi], preferred_element_type=jnp.float32))
    for j in range(N_LB):
        pbuf[j, 0:POOL_HALO, :] = pbuf[j, TM:TM + POOL_HALO, :]
    h = h + jnp.concatenate(mixed, axis=-1) * sc_ref[...]
    _to_token_tiles(o_ref, 0, h)

    hn2 = _rms(h, gf_ref[...])
    xh = hn2.astype(jnp.bfloat16)
    xl = (hn2 - xh.astype(jnp.float32)).astype(jnp.bfloat16)
    logits = (jnp.dot(xh, wrh_ref[...], preferred_element_type=jnp.float32)
              + jnp.dot(xl, wrh_ref[...], preferred_element_type=jnp.float32)
              + jnp.dot(xh, wrl_ref[...], preferred_element_type=jnp.float32))
    lane = lax.broadcasted_iota(jnp.int32, (TM, ROUTE_LANES), 1)
    neg = jnp.float32(-jnp.inf)
    lg = jnp.where(lane < N_EXPERTS, logits, neg)
    m1 = jnp.max(lg, axis=-1, keepdims=True)
    i1 = jnp.min(jnp.where(lg == m1, lane, ROUTE_LANES), axis=-1, keepdims=True)
    lg2 = jnp.where(lane == i1, neg, lg)
    m2 = jnp.max(lg2, axis=-1, keepdims=True)
    i2 = jnp.min(jnp.where(lg2 == m2, lane, ROUTE_LANES), axis=-1, keepdims=True)
    ex = jnp.exp(m2 - m1)
    w1 = 1.0 / (1.0 + ex)
    w2 = ex / (1.0 + ex)
    r = jnp.where(lane == 0, w1, 0.0)
    r = jnp.where(lane == 1, w2, r)
    r = jnp.where(lane == 2, i1.astype(jnp.float32), r)
    r = jnp.where(lane == 3, i2.astype(jnp.float32), r)
    r_ref[...] = r


def _pool_router(h, gm, wp, sc, gf, wrh, wrl):
    row = lambda i: (i, 0)
    return pl.pallas_call(
        _pool_router_kernel,
        out_shape=(jax.ShapeDtypeStruct((L * SUBLANES, LANES), jnp.float32),
                   jax.ShapeDtypeStruct((L, ROUTE_LANES), jnp.float32)),
        grid=(L // TM,),
        in_specs=[
            pl.BlockSpec((TM, D), row),
            _const_spec((1, D)),
            _const_spec((len(POOL_WINDOWS), POOL_GROUP, POOL_GROUP)),
            _const_spec((1, D)),
            _const_spec((1, D)),
            _const_spec((D, ROUTE_LANES)),
            _const_spec((D, ROUTE_LANES)),
        ],
        out_specs=(pl.BlockSpec((TM * SUBLANES, LANES), row), pl.BlockSpec((TM, ROUTE_LANES), row)),
        scratch_shapes=[pltpu.VMEM((N_LB, POOL_HALO + TM, LANES), jnp.float32)],
        compiler_params=pltpu.CompilerParams(
            dimension_semantics=("arbitrary",), vmem_limit_bytes=VMEM_LIMIT),
        name="pool_router",
    )(h, gm, wp, sc, gf, wrh, wrl)


def _moe_kernel(gexp_ref, gstart_ref, grows_ref, gsrc_ref, sdst_ref,
                h_hbm, g_ref, wg_ref, wu_ref, wd_ref, y_hbm,
                xg, xb, acc, ys, wgb, wub, wdb, gsem, ssem):
    del gexp_ref
    g = pl.program_id(0)
    c = pl.program_id(1)
    rows = grows_ref[g]
    nsub = (rows + SUB_ROWS - 1) // SUB_ROWS
    first = c == 0
    last = c == N_FF_CHUNKS - 1
    rows8 = lambda v: pl.ds(pl.multiple_of(v, SUBLANES), SUBLANES)

    def scatter_copy(st, j):
        return pltpu.make_async_copy(ys.at[rows8(j * SUBLANES)], y_hbm.at[rows8(sdst_ref[st + j])], ssem)

    def blocks_wait(sem, nblk):
        def wait_rows(n):
            span = pl.ds(0, n * SUBLANES)
            pltpu.make_async_copy(h_hbm.at[span], xg.at[span], sem).wait()

        def many(b, carry):
            wait_rows(WAIT_BLOCKS * DMA_UNROLL)
            return carry
        lax.fori_loop(0, nblk // WAIT_BLOCKS, many, 0)

        def one(b, carry):
            wait_rows(DMA_UNROLL)
            return carry
        lax.fori_loop(0, nblk % WAIT_BLOCKS, one, 0)

    def gather_start(grp):
        st = gstart_ref[grp]
        nblk = (grows_ref[grp] + DMA_UNROLL - 1) // DMA_UNROLL

        def body(b, carry):
            base = st + b * DMA_UNROLL
            nrow = DMA_UNROLL * SUBLANES
            dst = xg.at[pl.ds(pl.multiple_of(b * nrow, nrow), nrow)]
            for u in range(DMA_UNROLL):
                pltpu.make_async_copy(h_hbm.at[rows8(gsrc_ref[base + u])],
                                      dst.at[pl.ds(u * SUBLANES, SUBLANES)], gsem).start(priority=u % 2)
            return carry
        lax.fori_loop(0, nblk, body, 0)

    def gather_wait(grp):
        blocks_wait(gsem, (grows_ref[grp] + DMA_UNROLL - 1) // DMA_UNROLL)

    def scatter_start(grp):
        st = gstart_ref[grp]
        n = grows_ref[grp]
        nblk = n // DMA_UNROLL

        def body(b, carry):
            for u in range(DMA_UNROLL):
                scatter_copy(st, b * DMA_UNROLL + u).start(priority=u % 2)
            return carry
        lax.fori_loop(0, nblk, body, 0)

        def tail(j, carry):
            scatter_copy(st, j).start()
            return carry
        lax.fori_loop(nblk * DMA_UNROLL, n, tail, 0)

    def scatter_wait(grp):
        n = grows_ref[grp]
        nblk = n // DMA_UNROLL
        blocks_wait(ssem, nblk)

        def tail(j, carry):
            scatter_copy(0, 0).wait()
            return carry
        lax.fori_loop(nblk * DMA_UNROLL, n, tail, 0)

    gprev = jnp.maximum(g - 1, 0)
    gnext = jnp.minimum(g + 1, N_GROUPS - 1)
    prev_live = (g > 0) & (grows_ref[gprev] > 0)

    @pl.when(first & (g == 0) & (rows > 0))
    def _():
        xg[...] = jnp.zeros((GROUP_ROWS * SUBLANES, LANES), jnp.float32)
        gather_start(0)

    @pl.when(first & (rows > 0))
    def _():
        gather_wait(g)

        def norm(s):
            r0 = pl.multiple_of(s * SUB_ROWS, SUB_ROWS)
            x = _from_token_tiles(xg, r0, SUB_ROWS)
            xb[pl.ds(r0, SUB_ROWS), :] = _rms(x, g_ref[...]).astype(jnp.bfloat16)

        def norm_pair(k, carry):
            norm(2 * k)
            norm(2 * k + 1)
            return carry
        lax.fori_loop(0, nsub // 2, norm_pair, 0)

        @pl.when(nsub % 2 == 1)
        def _():
            norm(nsub - 1)

        @pl.when((g + 1 < N_GROUPS) & (grows_ref[gnext] > 0))
        def _():
            gather_start(gnext)

    def expert_out(s):
        r0 = pl.multiple_of(s * SUB_ROWS, SUB_ROWS)
        x = xb[pl.ds(r0, SUB_ROWS), :]
        a = jnp.dot(x, wgb[...], preferred_element_type=jnp.float32)
        b = jnp.dot(x, wub[...], preferred_element_type=jnp.float32)
        mid = (a * jax.nn.sigmoid(a) * b).astype(jnp.bfloat16)
        return r0, jnp.dot(mid, wdb[...], preferred_element_type=jnp.float32)

    @pl.when(rows > 0)
    def _():
        wgb[...] = wg_ref[0].astype(jnp.bfloat16)
        wub[...] = wu_ref[0].astype(jnp.bfloat16)
        wdb[...] = wd_ref[0].astype(jnp.bfloat16)

    def chunk_pass(update):
        def pair(k, carry):
            update(*expert_out(2 * k))
            update(*expert_out(2 * k + 1))
            return carry
        lax.fori_loop(0, nsub // 2, pair, 0)

        @pl.when(nsub % 2 == 1)
        def _():
            update(*expert_out(nsub - 1))

    @pl.when(first & (rows > 0))
    def _():
        def update(r0, y):
            acc[pl.ds(r0, SUB_ROWS), :] = y
        chunk_pass(update)

    @pl.when(jnp.logical_not(first) & jnp.logical_not(last) & (rows > 0))
    def _():
        def update(r0, y):
            acc[pl.ds(r0, SUB_ROWS), :] += y
        chunk_pass(update)

    @pl.when(last & prev_live)
    def _():
        scatter_wait(gprev)

    @pl.when(last & (rows > 0))
    def _():
        def update(r0, y):
            _to_token_tiles(ys, r0, acc[pl.ds(r0, SUB_ROWS), :] + y)
        chunk_pass(update)
        scatter_start(g)

        @pl.when(g == N_GROUPS - 1)
        def _():
            scatter_wait(g)


def _moe(gexp, gstart, grows, gsrc, sdst, h, g, wg, wu, wd):
    def w_in(gi, ci, gexp, gstart, grows, *_):
        return (gexp[gi], 0, jnp.where(grows[gi] > 0, ci, N_FF_CHUNKS - 1))

    def w_out(gi, ci, gexp, gstart, grows, *_):
        return (gexp[gi], jnp.where(grows[gi] > 0, ci, N_FF_CHUNKS - 1), 0)

    grid_spec = pltpu.PrefetchScalarGridSpec(
        num_scalar_prefetch=5,
        grid=(N_GROUPS, N_FF_CHUNKS),
        in_specs=[
            pl.BlockSpec(memory_space=pl.ANY),
            pl.BlockSpec((1, D), lambda gi, ci, *_: (0, 0)),
            pl.BlockSpec((1, D, FF_CHUNK), w_in),
            pl.BlockSpec((1, D, FF_CHUNK), w_in),
            pl.BlockSpec((1, FF_CHUNK, D), w_out),
        ],
        out_specs=pl.BlockSpec(memory_space=pl.ANY),
        scratch_shapes=[
            pltpu.VMEM((GROUP_ROWS * SUBLANES, LANES), jnp.float32),
            pltpu.VMEM((GROUP_ROWS, D), jnp.bfloat16),
            pltpu.VMEM((GROUP_ROWS, D), jnp.float32),
            pltpu.VMEM((GROUP_ROWS * SUBLANES, LANES), jnp.float32),
            pltpu.VMEM((D, FF_CHUNK), jnp.bfloat16),
            pltpu.VMEM((D, FF_CHUNK), jnp.bfloat16),
            pltpu.VMEM((FF_CHUNK, D), jnp.bfloat16),
            pltpu.SemaphoreType.DMA(()),
            pltpu.SemaphoreType.DMA(()),
        ],
    )
    return pl.pallas_call(
        _moe_kernel,
        out_shape=jax.ShapeDtypeStruct((N_ASSIGN * SUBLANES, LANES), jnp.float32),
        grid_spec=grid_spec,
        compiler_params=pltpu.CompilerParams(
            dimension_semantics=("arbitrary", "arbitrary"), vmem_limit_bytes=VMEM_LIMIT),
        name="moe_experts",
    )(gexp, gstart, grows, gsrc, sdst, h, g, wg, wu, wd)


def _final_kernel(h_ref, y0_ref, y1_ref, r_ref, g_ref, o_ref):
    r = r_ref[...]
    h = (_from_token_tiles(h_ref, 0, TF)
         + r[:, 0:1] * _from_token_tiles(y0_ref, 0, TF)
         + r[:, 1:2] * _from_token_tiles(y1_ref, 0, TF))
    o_ref[...] = _rms(h, g_ref[...])


def _final(h, y, r, g):
    return pl.pallas_call(
        _final_kernel,
        out_shape=jax.ShapeDtypeStruct((SEQ, D), jnp.float32),
        grid=(SEQ // TF,),
        in_specs=[
            pl.BlockSpec((pl.Element(TF * SUBLANES), pl.Element(LANES)),
                         lambda i: ((N_META + i * TF) * SUBLANES, 0)),
            pl.BlockSpec((TF * SUBLANES, LANES), lambda i: (i, 0)),
            pl.BlockSpec((TF * SUBLANES, LANES), lambda i: (i + SEQ // TF, 0)),
            pl.BlockSpec((pl.Element(TF), pl.Element(ROUTE_LANES)),
                         lambda i: (SUBLANES * (N_META // SUBLANES + i * (TF // SUBLANES)), 0)),
            _const_spec((1, D)),
        ],
        out_specs=pl.BlockSpec((TF, D), lambda i: (i, 0)),
        compiler_params=pltpu.CompilerParams(
            dimension_semantics=("arbitrary",), vmem_limit_bytes=VMEM_LIMIT),
        name="combine_norm",
    )(h, y, y, r, g)


def _group_table(flat_e):
    order = jnp.argsort(flat_e, stable=True).astype(jnp.int32)
    order = jnp.concatenate([order, jnp.zeros((DMA_UNROLL,), jnp.int32)])
    gsrc = (N_META + (order & (SEQ - 1))) * SUBLANES
    sdst = order * SUBLANES
    experts = jnp.arange(N_EXPERTS, dtype=jnp.int32)
    counts = jnp.sum(flat_e[:, None] == experts[None, :], axis=0, dtype=jnp.int32)
    ngrp = (counts + GROUP_ROWS - 1) // GROUP_ROWS
    cum_grp = jnp.cumsum(ngrp)
    total = cum_grp[-1]
    first_row = jnp.cumsum(counts) - counts
    gi = jnp.arange(N_GROUPS, dtype=jnp.int32)
    valid = gi < total
    gclip = jnp.minimum(gi, total - 1)
    e = jnp.sum(gclip[:, None] >= cum_grp[None, :], axis=1, dtype=jnp.int32)
    e = jnp.clip(e, 0, N_EXPERTS - 1)
    local = gclip - (cum_grp[e] - ngrp[e])
    gstart = jnp.where(valid, first_row[e] + local * GROUP_ROWS, 0).astype(jnp.int32)
    grows = jnp.where(valid, jnp.clip(counts[e] - local * GROUP_ROWS, 0, GROUP_ROWS), 0).astype(jnp.int32)
    return gsrc, sdst, e, gstart, grows


def kernel(x, meta_tokens, conv_w_pw1, conv_b_pw1, conv_w_dw, conv_b_dw, conv_ln_g, conv_ln_b,
           conv_w_pw2, conv_b_pw2, pool_w_group, pool_scale, ffn_w_gate, ffn_w_up, ffn_w_down,
           moe_w_router, moe_w_gate, moe_w_up, moe_w_down, mix_norm_g, ffn_norm_g, final_norm_g):
    bf = jnp.bfloat16
    h = _conv_mixer(x[0], meta_tokens.astype(x.dtype), mix_norm_g[0:1], conv_w_pw1[0].astype(bf),
                    conv_b_pw1, conv_w_dw[0], conv_b_dw, conv_ln_g, conv_ln_b,
                    conv_w_pw2[0].astype(bf), conv_b_pw2)
    h = _dense_ffn(h, ffn_norm_g[0:1], ffn_w_gate[0].astype(bf), ffn_w_up[0].astype(bf),
                   ffn_w_down[0].astype(bf))

    wr = jnp.pad(moe_w_router[0], ((0, 0), (0, ROUTE_LANES - N_EXPERTS)))
    wrh = wr.astype(bf)
    wrl = (wr - wrh.astype(jnp.float32)).astype(bf)
    ht, route = _pool_router(h, mix_norm_g[1:2], pool_w_group[0].astype(bf), pool_scale,
                             ffn_norm_g[1:2], wrh, wrl)

    flat_e = jnp.concatenate([route[N_META:, 2], route[N_META:, 3]]).astype(jnp.int32)
    gsrc, sdst, gexp, gstart, grows = _group_table(flat_e)
    y = _moe(gexp, gstart, grows, gsrc, sdst, ht, ffn_norm_g[1:2], moe_w_gate[0], moe_w_up[0], moe_w_down[0])

    out = _final(ht, y, route, final_norm_g[None, :])
    return out[None]
```

```python
import jax
import jax.numpy as jnp
from jax import lax
from jax.experimental import pallas as pl
from jax.experimental.pallas import tpu as pltpu

D = 1024
SEQ = 16384
N_META = 16
L = SEQ + N_META
CONV_WIDTH = 31
POOL_WINDOWS = (2, 4, 8, 16)
POOL_GROUP = D // len(POOL_WINDOWS)
D_FF = 2816
N_EXPERTS = 8
D_FF_EXPERT = 3584
RMS_EPS = 1e-6
LN_EPS = 1e-5

LANES = 128
SUBLANES = 8
N_LB = D // LANES
TM = 656
CONV_HALO = 32
CONV_ROWS = 16
CONV_PARTS = 4
CONV_UNROLL = 8
ROW_SPLIT = ((0, 320), (320, TM))
POOL_HALO = 16
ROUTE_LANES = LANES
GROUP_ROWS = 2560
SUB_ROWS = 640
FF_CHUNK = 512
N_FF_CHUNKS = D_FF_EXPERT // FF_CHUNK
N_ASSIGN = 2 * SEQ
N_GROUPS = (N_ASSIGN + N_EXPERTS * (GROUP_ROWS - 1)) // GROUP_ROWS
DMA_UNROLL = 32
WAIT_BLOCKS = 4
TF = 1024
VMEM_LIMIT = 56 * 1024 * 1024

assert SEQ & (SEQ - 1) == 0 and L % TM == 0 and SEQ % TF == 0 and GROUP_ROWS % SUB_ROWS == 0
assert GROUP_ROWS % DMA_UNROLL == 0 and D_FF_EXPERT % FF_CHUNK == 0 and N_LB == SUBLANES


def _rms(x, g):
    return x * lax.rsqrt(jnp.mean(x * x, axis=-1, keepdims=True) + RMS_EPS) * g


def _const_spec(shape):
    zeros = (0,) * len(shape)
    return pl.BlockSpec(shape, lambda *_: zeros, pipeline_mode=pl.Buffered(1))


def _to_token_tiles(ref, row0, x):
    n = x.shape[0]
    for j in range(N_LB):
        ref[pl.ds(row0 * SUBLANES + j, n, stride=SUBLANES), :] = x[:, j * LANES:(j + 1) * LANES]


def _from_token_tiles(ref, row0, n):
    return jnp.concatenate(
        [ref[pl.ds(row0 * SUBLANES + j, n, stride=SUBLANES), :] for j in range(N_LB)], axis=1)


def _conv_mixer_kernel(x_ref, meta_ref, g_ref, w1_ref, b1_ref, wdw_ref, bdw_ref, lng_ref, lnb_ref,
                       w2_ref, b2_ref, o_ref, hbuf, ubuf, cbuf, wbc):
    i = pl.program_id(0)

    @pl.when(i == 0)
    def _():
        ubuf[:, 0:CONV_HALO, :] = jnp.zeros((N_LB, CONV_HALO, LANES), jnp.float32)
        for k in range(CONV_WIDTH):
            wbc[k * SUBLANES:(k + 1) * SUBLANES, :] = jnp.broadcast_to(wdw_ref[k:k + 1, :], (SUBLANES, D))
        hbuf[0:N_META, :] = meta_ref[...]
        hbuf[N_META:TM, :] = x_ref[0:TM - N_META, :]

    @pl.when(i > 0)
    def _():
        hbuf[...] = x_ref[...]

    for lo, hi in ROW_SPLIT:
        hn = _rms(hbuf[lo:hi, :], g_ref[...]).astype(jnp.bfloat16)
        u = jnp.dot(hn, w1_ref[...], preferred_element_type=jnp.float32) + b1_ref[...]
        u = u[:, :D] * jax.nn.sigmoid(u[:, D:])
        for j in range(N_LB):
            ubuf[j, CONV_HALO + lo:CONV_HALO + hi, :] = u[:, j * LANES:(j + 1) * LANES]

    for j in range(N_LB):
        lanes = slice(j * LANES, (j + 1) * LANES)
        taps = [wbc[k * SUBLANES:(k + 1) * SUBLANES, lanes] for k in range(CONV_WIDTH)]
        bias = jnp.broadcast_to(bdw_ref[:, lanes], (SUBLANES, LANES))

        def chunk(r, carry, j=j, lanes=lanes, taps=taps, bias=bias):
            base = pl.multiple_of(r * CONV_ROWS, CONV_ROWS)
            for q in range(CONV_ROWS // SUBLANES):
                parts = [None] * CONV_PARTS
                for k in range(CONV_WIDTH):
                    d = CONV_WIDTH - 1 - k
                    term = ubuf[j, pl.ds(CONV_HALO + base + q * SUBLANES - d, SUBLANES), :] * taps[k]
                    parts[k % CONV_PARTS] = term if parts[k % CONV_PARTS] is None else parts[k % CONV_PARTS] + term
                cbuf[pl.ds(base + q * SUBLANES, SUBLANES), lanes] = (
                    (parts[0] + parts[1]) + (parts[2] + parts[3]) + bias)
            return carry

        n_chunks = TM // CONV_ROWS
        n_looped = n_chunks - n_chunks % CONV_UNROLL
        lax.fori_loop(0, n_looped, chunk, 0, unroll=CONV_UNROLL)
        for r in range(n_looped, n_chunks):
            chunk(r, 0)
        ubuf[j, 0:CONV_HALO, :] = ubuf[j, TM:TM + CONV_HALO, :]

    for lo, hi in ROW_SPLIT:
        c = cbuf[lo:hi, :]
        mu = jnp.mean(c, axis=-1, keepdims=True)
        cc = c - mu
        var = jnp.mean(cc * cc, axis=-1, keepdims=True)
        y = cc * lax.rsqrt(var + LN_EPS) * lng_ref[...] + lnb_ref[...]
        y = (y * jax.nn.sigmoid(y)).astype(jnp.bfloat16)
        o_ref[lo:hi, :] = (hbuf[lo:hi, :] + jnp.dot(y, w2_ref[...], preferred_element_type=jnp.float32)
                           + b2_ref[...])


def _conv_mixer(x, meta, g, w1, b1, wdw, bdw, lng, lnb, w2, b2):
    row = lambda i: (i, 0)
    return pl.pallas_call(
        _conv_mixer_kernel,
        out_shape=jax.ShapeDtypeStruct((L, D), jnp.float32),
        grid=(L // TM,),
        in_specs=[
            pl.BlockSpec((pl.Element(TM), pl.Element(D)),
                         lambda i: (SUBLANES * jnp.maximum(i * (TM // SUBLANES) - N_META // SUBLANES, 0), 0)),
            _const_spec((N_META, D)),
            _const_spec((1, D)),
            _const_spec((D, 2 * D)),
            _const_spec((1, 2 * D)),
            _const_spec((CONV_WIDTH, D)),
            _const_spec((1, D)),
            _const_spec((1, D)),
            _const_spec((1, D)),
            _const_spec((D, D)),
            _const_spec((1, D)),
        ],
        out_specs=pl.BlockSpec((TM, D), row),
        scratch_shapes=[
            pltpu.VMEM((TM, D), jnp.float32),
            pltpu.VMEM((N_LB, CONV_HALO + TM, LANES), jnp.float32),
            pltpu.VMEM((TM, D), jnp.float32),
            pltpu.VMEM((CONV_WIDTH * SUBLANES, D), jnp.float32),
        ],
        compiler_params=pltpu.CompilerParams(
            dimension_semantics=("arbitrary",), vmem_limit_bytes=VMEM_LIMIT),
        name="conv_mixer",
    )(x, meta, g, w1, b1, wdw, bdw, lng, lnb, w2, b2)


def _dense_ffn_kernel(h_ref, g_ref, wg_ref, wu_ref, wd_ref, o_ref):
    for lo, hi in ROW_SPLIT:
        h = h_ref[lo:hi, :]
        hn = _rms(h, g_ref[...]).astype(jnp.bfloat16)
        a = jnp.dot(hn, wg_ref[...], preferred_element_type=jnp.float32)
        b = jnp.dot(hn, wu_ref[...], preferred_element_type=jnp.float32)
        mid = (a * jax.nn.sigmoid(a) * b).astype(jnp.bfloat16)
        o_ref[lo:hi, :] = h + jnp.dot(mid, wd_ref[...], preferred_element_type=jnp.float32)


def _dense_ffn(h, g, wg, wu, wd):
    row = lambda i: (i, 0)
    return pl.pallas_call(
        _dense_ffn_kernel,
        out_shape=jax.ShapeDtypeStruct((L, D), jnp.float32),
        grid=(L // TM,),
        in_specs=[
            pl.BlockSpec((TM, D), row),
            _const_spec((1, D)),
            _const_spec((D, D_FF)),
            _const_spec((D, D_FF)),
            _const_spec((D_FF, D)),
        ],
        out_specs=pl.BlockSpec((TM, D), row),
        compiler_params=pltpu.CompilerParams(
            dimension_semantics=("arbitrary",), vmem_limit_bytes=VMEM_LIMIT),
        name="dense_ffn",
    )(h, g, wg, wu, wd)


def _pool_router_kernel(h_ref, gm_ref, wp_ref, sc_ref, gf_ref, wrh_ref, wrl_ref, o_ref, r_ref, pbuf):
    i = pl.program_id(0)

    @pl.when(i == 0)
    def _():
        pbuf[:, 0:POOL_HALO, :] = jnp.zeros((N_LB, POOL_HALO, LANES), jnp.float32)

    h = h_ref[...]
    hn = _rms(h, gm_ref[...])
    for j in range(N_LB):
        pbuf[j, POOL_HALO:POOL_HALO + TM, :] = hn[:, j * LANES:(j + 1) * LANES]
    pos = i * TM + lax.broadcasted_iota(jnp.int32, (TM, 1), 0)
    lb_per_group = POOL_GROUP // LANES
    mixed = []
    for gi, w in enumerate(POOL_WINDOWS):
        inv = 1.0 / jnp.minimum(pos + 1, w).astype(jnp.float32)
        cols = []
        for j in range(gi * lb_per_group, (gi + 1) * lb_per_group):
            x = hn[:, j * LANES:(j + 1) * LANES]
            s = x
            for d in range(1, w):
                s = s + pbuf[j, POOL_HALO - d:POOL_HALO - d + TM, :]
            cols.append(s * inv - x)
        pooled = jnp.concatenate(cols, axis=1).astype(jnp.bfloat16)
        mixed.append(jnp.dot(pooled, wp_ref[gi], preferred_element_type=jnp.float32))
    for j in range(N_LB):
        pbuf[j, 0:POOL_HALO, :] = pbuf[j, TM:TM + POOL_HALO, :]
    h = h + jnp.concatenate(mixed, axis=-1) * sc_ref[...]
    _to_token_tiles(o_ref, 0, h)

    hn2 = _rms(h, gf_ref[...])
    xh = hn2.astype(jnp.bfloat16)
    xl = (hn2 - xh.astype(jnp.float32)).astype(jnp.bfloat16)
    logits = (jnp.dot(xh, wrh_ref[...], preferred_element_type=jnp.float32)
              + jnp.dot(xl, wrh_ref[...], preferred_element_type=jnp.float32)
              + jnp.dot(xh, wrl_ref[...], preferred_element_type=jnp.float32))
    lane = lax.broadcasted_iota(jnp.int32, (TM, ROUTE_LANES), 1)
    neg = jnp.float32(-jnp.inf)
    lg = jnp.where(lane < N_EXPERTS, logits, neg)
    m1 = jnp.max(lg, axis=-1, keepdims=True)
    i1 = jnp.min(jnp.where(lg == m1, lane, ROUTE_LANES), axis=-1, keepdims=True)
    lg2 = jnp.where(lane == i1, neg, lg)
    m2 = jnp.max(lg2, axis=-1, keepdims=True)
    i2 = jnp.min(jnp.where(lg2 == m2, lane, ROUTE_LANES), axis=-1, keepdims=True)
    ex = jnp.exp(m2 - m1)
    w1 = 1.0 / (1.0 + ex)
    w2 = ex / (1.0 + ex)
    r = jnp.where(lane == 0, w1, 0.0)
    r = jnp.where(lane == 1, w2, r)
    r = jnp.where(lane == 2, i1.astype(jnp.float32), r)
    r = jnp.where(lane == 3, i2.astype(jnp.float32), r)
    r_ref[...] = r


def _pool_router(h, gm, wp, sc, gf, wrh, wrl):
    row = lambda i: (i, 0)
    return pl.pallas_call(
        _pool_router_kernel,
        out_shape=(jax.ShapeDtypeStruct((L * SUBLANES, LANES), jnp.float32),
                   jax.ShapeDtypeStruct((L, ROUTE_LANES), jnp.float32)),
        grid=(L // TM,),
        in_specs=[
            pl.BlockSpec((TM, D), row),
            _const_spec((1, D)),
            _const_spec((len(POOL_WINDOWS), POOL_GROUP, POOL_GROUP)),
            _const_spec((1, D)),
            _const_spec((1, D)),
            _const_spec((D, ROUTE_LANES)),
            _const_spec((D, ROUTE_LANES)),
        ],
        out_specs=(pl.BlockSpec((TM * SUBLANES, LANES), row), pl.BlockSpec((TM, ROUTE_LANES), row)),
        scratch_shapes=[pltpu.VMEM((N_LB, POOL_HALO + TM, LANES), jnp.float32)],
        compiler_params=pltpu.CompilerParams(
            dimension_semantics=("arbitrary",), vmem_limit_bytes=VMEM_LIMIT),
        name="pool_router",
    )(h, gm, wp, sc, gf, wrh, wrl)


def _moe_kernel(gexp_ref, gstart_ref, grows_ref, gsrc_ref, sdst_ref,
                h_hbm, g_ref, wg_ref, wu_ref, wd_ref, y_hbm,
                xg, xb, acc, ys, wgb, wub, wdb, gsem, ssem):
    del gexp_ref
    g = pl.program_id(0)
    c = pl.program_id(1)
    rows = grows_ref[g]
    nsub = (rows + SUB_ROWS - 1) // SUB_ROWS
    first = c == 0
    last = c == N_FF_CHUNKS - 1
    rows8 = lambda v: pl.ds(pl.multiple_of(v, SUBLANES), SUBLANES)

    def scatter_copy(st, j):
        return pltpu.make_async_copy(ys.at[rows8(j * SUBLANES)], y_hbm.at[rows8(sdst_ref[st + j])], ssem)

    def blocks_wait(sem, nblk):
        def wait_rows(n):
            span = pl.ds(0, n * SUBLANES)
            pltpu.make_async_copy(h_hbm.at[span], xg.at[span], sem).wait()

        def many(b, carry):
            wait_rows(WAIT_BLOCKS * DMA_UNROLL)
            return carry
        lax.fori_loop(0, nblk // WAIT_BLOCKS, many, 0)

        def one(b, carry):
            wait_rows(DMA_UNROLL)
            return carry
        lax.fori_loop(0, nblk % WAIT_BLOCKS, one, 0)

    def gather_start(grp):
        st = gstart_ref[grp]
        nblk = (grows_ref[grp] + DMA_UNROLL - 1) // DMA_UNROLL

        def body(b, carry):
            base = st + b * DMA_UNROLL
            nrow = DMA_UNROLL * SUBLANES
            dst = xg.at[pl.ds(pl.multiple_of(b * nrow, nrow), nrow)]
            for u in range(DMA_UNROLL):
                pltpu.make_async_copy(h_hbm.at[rows8(gsrc_ref[base + u])],
                                      dst.at[pl.ds(u * SUBLANES, SUBLANES)], gsem).start(priority=u % 2)
            return carry
        lax.fori_loop(0, nblk, body, 0)

    def gather_wait(grp):
        blocks_wait(gsem, (grows_ref[grp] + DMA_UNROLL - 1) // DMA_UNROLL)

    def scatter_start(grp):
        st = gstart_ref[grp]
        n = grows_ref[grp]
        nblk = n // DMA_UNROLL

        def body(b, carry):
            for u in range(DMA_UNROLL):
                scatter_copy(st, b * DMA_UNROLL + u).start(priority=u % 2)
            return carry
        lax.fori_loop(0, nblk, body, 0)

        def tail(j, carry):
            scatter_copy(st, j).start()
            return carry
        lax.fori_loop(nblk * DMA_UNROLL, n, tail, 0)

    def scatter_wait(grp):
        n = grows_ref[grp]
        nblk = n // DMA_UNROLL
        blocks_wait(ssem, nblk)

        def tail(j, carry):
            scatter_copy(0, 0).wait()
            return carry
        lax.fori_loop(nblk * DMA_UNROLL, n, tail, 0)

    gprev = jnp.maximum(g - 1, 0)
    gnext = jnp.minimum(g + 1, N_GROUPS - 1)
    prev_live = (g > 0) & (grows_ref[gprev] > 0)

    @pl.when(first & (g == 0) & (rows > 0))
    def _():
        xg[...] = jnp.zeros((GROUP_ROWS * SUBLANES, LANES), jnp.float32)
        gather_start(0)

    @pl.when(first & (rows > 0))
    def _():
        gather_wait(g)

        def norm(s):
            r0 = pl.multiple_of(s * SUB_ROWS, SUB_ROWS)
            x = _from_token_tiles(xg, r0, SUB_ROWS)
            xb[pl.ds(r0, SUB_ROWS), :] = _rms(x, g_ref[...]).astype(jnp.bfloat16)
            acc[pl.ds(r0, SUB_ROWS), :] = jnp.zeros((SUB_ROWS, D), jnp.float32)

        def norm_pair(k, carry):
            norm(2 * k)
            norm(2 * k + 1)
            return carry
        lax.fori_loop(0, nsub // 2, norm_pair, 0)

        @pl.when(nsub % 2 == 1)
        def _():
            norm(nsub - 1)

        @pl.when((g + 1 < N_GROUPS) & (grows_ref[gnext] > 0))
        def _():
            gather_start(gnext)

    def expert_out(s):
        r0 = pl.multiple_of(s * SUB_ROWS, SUB_ROWS)
        x = xb[pl.ds(r0, SUB_ROWS), :]
        a = jnp.dot(x, wgb[...], preferred_element_type=jnp.float32)
        b = jnp.dot(x, wub[...], preferred_element_type=jnp.float32)
        mid = (a * jax.nn.sigmoid(a) * b).astype(jnp.bfloat16)
        return r0, jnp.dot(mid, wdb[...], preferred_element_type=jnp.float32)

    @pl.when(rows > 0)
    def _():
        wgb[...] = wg_ref[0].astype(jnp.bfloat16)
        wub[...] = wu_ref[0].astype(jnp.bfloat16)
        wdb[...] = wd_ref[0].astype(jnp.bfloat16)

    def chunk_pass(update):
        def pair(k, carry):
            update(*expert_out(2 * k))
            update(*expert_out(2 * k + 1))
            return carry
        lax.fori_loop(0, nsub // 2, pair, 0)

        @pl.when(nsub % 2 == 1)
        def _():
            update(*expert_out(nsub - 1))

    @pl.when(jnp.logical_not(last) & (rows > 0))
    def _():
        def update(r0, y):
            acc[pl.ds(r0, SUB_ROWS), :] += y
        chunk_pass(update)

    @pl.when(last & prev_live)
    def _():
        scatter_wait(gprev)

    @pl.when(last & (rows > 0))
    def _():
        def update(r0, y):
            _to_token_tiles(ys, r0, acc[pl.ds(r0, SUB_ROWS), :] + y)
        chunk_pass(update)
        scatter_start(g)

        @pl.when(g == N_GROUPS - 1)
        def _():
            scatter_wait(g)


def _moe(gexp, gstart, grows, gsrc, sdst, h, g, wg, wu, wd):
    def w_in(gi, ci, gexp, gstart, grows, *_):
        return (gexp[gi], 0, jnp.where(grows[gi] > 0, ci, N_FF_CHUNKS - 1))

    def w_out(gi, ci, gexp, gstart, grows, *_):
        return (gexp[gi], jnp.where(grows[gi] > 0, ci, N_FF_CHUNKS - 1), 0)

    grid_spec = pltpu.PrefetchScalarGridSpec(
        num_scalar_prefetch=5,
        grid=(N_GROUPS, N_FF_CHUNKS),
        in_specs=[
            pl.BlockSpec(memory_space=pl.ANY),
            pl.BlockSpec((1, D), lambda gi, ci, *_: (0, 0)),
            pl.BlockSpec((1, D, FF_CHUNK), w_in),
            pl.BlockSpec((1, D, FF_CHUNK), w_in),
            pl.BlockSpec((1, FF_CHUNK, D), w_out),
        ],
        out_specs=pl.BlockSpec(memory_space=pl.ANY),
        scratch_shapes=[
            pltpu.VMEM((GROUP_ROWS * SUBLANES, LANES), jnp.float32),
            pltpu.VMEM((GROUP_ROWS, D), jnp.bfloat16),
            pltpu.VMEM((GROUP_ROWS, D), jnp.float32),
            pltpu.VMEM((GROUP_ROWS * SUBLANES, LANES), jnp.float32),
            pltpu.VMEM((D, FF_CHUNK), jnp.bfloat16),
            pltpu.VMEM((D, FF_CHUNK), jnp.bfloat16),
            pltpu.VMEM((FF_CHUNK, D), jnp.bfloat16),
            pltpu.SemaphoreType.DMA(()),
            pltpu.SemaphoreType.DMA(()),
        ],
    )
    return pl.pallas_call(
        _moe_kernel,
        out_shape=jax.ShapeDtypeStruct((N_ASSIGN * SUBLANES, LANES), jnp.float32),
        grid_spec=grid_spec,
        compiler_params=pltpu.CompilerParams(
            dimension_semantics=("arbitrary", "arbitrary"), vmem_limit_bytes=VMEM_LIMIT),
        name="moe_experts",
    )(gexp, gstart, grows, gsrc, sdst, h, g, wg, wu, wd)


def _final_kernel(h_ref, y0_ref, y1_ref, r_ref, g_ref, o_ref):
    r = r_ref[...]
    h = (_from_token_tiles(h_ref, 0, TF)
         + r[:, 0:1] * _from_token_tiles(y0_ref, 0, TF)
         + r[:, 1:2] * _from_token_tiles(y1_ref, 0, TF))
    o_ref[...] = _rms(h, g_ref[...])


def _final(h, y, r, g):
    return pl.pallas_call(
        _final_kernel,
        out_shape=jax.ShapeDtypeStruct((SEQ, D), jnp.float32),
        grid=(SEQ // TF,),
        in_specs=[
            pl.BlockSpec((pl.Element(TF * SUBLANES), pl.Element(LANES)),
                         lambda i: ((N_META + i * TF) * SUBLANES, 0)),
            pl.BlockSpec((TF * SUBLANES, LANES), lambda i: (i, 0)),
            pl.BlockSpec((TF * SUBLANES, LANES), lambda i: (i + SEQ // TF, 0)),
            pl.BlockSpec((pl.Element(TF), pl.Element(ROUTE_LANES)),
                         lambda i: (SUBLANES * (N_META // SUBLANES + i * (TF // SUBLANES)), 0)),
            _const_spec((1, D)),
        ],
        out_specs=pl.BlockSpec((TF, D), lambda i: (i, 0)),
        compiler_params=pltpu.CompilerParams(
            dimension_semantics=("arbitrary",), vmem_limit_bytes=VMEM_LIMIT),
        name="combine_norm",
    )(h, y, y, r, g)


def _group_table(flat_e):
    order = jnp.argsort(flat_e, stable=True).astype(jnp.int32)
    order = jnp.concatenate([order, jnp.zeros((DMA_UNROLL,), jnp.int32)])
    gsrc = (N_META + (order & (SEQ - 1))) * SUBLANES
    sdst = order * SUBLANES
    experts = jnp.arange(N_EXPERTS, dtype=jnp.int32)
    counts = jnp.sum(flat_e[:, None] == experts[None, :], axis=0, dtype=jnp.int32)
    ngrp = (counts + GROUP_ROWS - 1) // GROUP_ROWS
    cum_grp = jnp.cumsum(ngrp)
    total = cum_grp[-1]
    first_row = jnp.cumsum(counts) - counts
    gi = jnp.arange(N_GROUPS, dtype=jnp.int32)
    valid = gi < total
    gclip = jnp.minimum(gi, total - 1)
    e = jnp.sum(gclip[:, None] >= cum_grp[None, :], axis=1, dtype=jnp.int32)
    e = jnp.clip(e, 0, N_EXPERTS - 1)
    local = gclip - (cum_grp[e] - ngrp[e])
    gstart = jnp.where(valid, first_row[e] + local * GROUP_ROWS, 0).astype(jnp.int32)
    grows = jnp.where(valid, jnp.clip(counts[e] - local * GROUP_ROWS, 0, GROUP_ROWS), 0).astype(jnp.int32)
    return gsrc, sdst, e, gstart, grows


def kernel(x, meta_tokens, conv_w_pw1, conv_b_pw1, conv_w_dw, conv_b_dw, conv_ln_g, conv_ln_b,
           conv_w_pw2, conv_b_pw2, pool_w_group, pool_scale, ffn_w_gate, ffn_w_up, ffn_w_down,
           moe_w_router, moe_w_gate, moe_w_up, moe_w_down, mix_norm_g, ffn_norm_g, final_norm_g):
    bf = jnp.bfloat16
    h = _conv_mixer(x[0], meta_tokens.astype(x.dtype), mix_norm_g[0:1], conv_w_pw1[0].astype(bf),
                    conv_b_pw1, conv_w_dw[0], conv_b_dw, conv_ln_g, conv_ln_b,
                    conv_w_pw2[0].astype(bf), conv_b_pw2)
    h = _dense_ffn(h, ffn_norm_g[0:1], ffn_w_gate[0].astype(bf), ffn_w_up[0].astype(bf),
                   ffn_w_down[0].astype(bf))

    wr = jnp.pad(moe_w_router[0], ((0, 0), (0, ROUTE_LANES - N_EXPERTS)))
    wrh = wr.astype(bf)
    wrl = (wr - wrh.astype(jnp.float32)).astype(bf)
    ht, route = _pool_router(h, mix_norm_g[1:2], pool_w_group[0].astype(bf), pool_scale,
                             ffn_norm_g[1:2], wrh, wrl)

    flat_e = jnp.concatenate([route[N_META:, 2], route[N_META:, 3]]).astype(jnp.int32)
    gsrc, sdst, gexp, gstart, grows = _group_table(flat_e)
    y = _moe(gexp, gstart, grows, gsrc, sdst, ht, ffn_norm_g[1:2], moe_w_gate[0], moe_w_up[0], moe_w_down[0])

    out = _final(ht, y, route, final_norm_g[None, :])
    return out[None]
```
